```python
import math
import jax, jax.numpy as jnp
from jax import lax
import numpy as np

D_MODEL = 1024
BATCH = 2
SEQ = 16384
DEPTH = 4

A_HEADS = 8
A_KV_HEADS = 2
A_HEAD_DIM = 64
A_WINDOW = 128
ROPE_THETA = 500000.0
A_ROPE_DIM = A_HEAD_DIM // 4
B_CHANNELS = D_MODEL // 2
B_CONV = 31
C_HEADS = 16
C_HEAD_DIM = 64
C_INNER = C_HEADS * C_HEAD_DIM
C_GROUPS = 2
C_STATE = 128
C_CONV = 4
C_CHUNK = 128
D_HEADS = 4
D_QK_DIM = 128
D_V_DIM = 256
D_CHUNK = 128
RET_THETA = 10000.0
FFN_HIDDEN = -(-(8 * D_MODEL) // (3 * 256)) * 256
ALPHA = (2 * DEPTH) ** 0.25
BETA = (8 * DEPTH) ** -0.25
NORM_EPS = 1e-5

N_EVEN = (DEPTH + 1) // 2
N_ODD = DEPTH // 2

EVEN_SPLITS = (A_HEADS * A_HEAD_DIM, A_KV_HEADS * A_HEAD_DIM, A_KV_HEADS * A_HEAD_DIM, 2 * B_CHANNELS)
EVEN_IN = sum(EVEN_SPLITS)
EVEN_MIX = A_HEADS * A_HEAD_DIM + B_CHANNELS
C_XBC = C_INNER + 2 * C_GROUPS * C_STATE
ODD_SPLITS = (C_INNER, C_XBC, C_HEADS, D_HEADS * D_QK_DIM, D_HEADS * D_QK_DIM, D_HEADS * D_V_DIM, D_HEADS * D_V_DIM)
ODD_IN = sum(ODD_SPLITS)
ODD_MIX = C_INNER + D_HEADS * D_V_DIM

kernel_name = "hybrid_swa_conformer_mamba2_retention_deepnorm"


def _split(t, sizes):
    idx = [int(v) for v in np.cumsum(sizes)[:-1]]
    return jnp.split(t, idx, axis=-1)


def layer_norm(x, g, b):
    xf = x.astype(jnp.float32)
    mu = jnp.mean(xf, axis=-1, keepdims=True)
    var = jnp.mean(jnp.square(xf - mu), axis=-1, keepdims=True)
    y = (xf - mu) * lax.rsqrt(var + NORM_EPS) * g.astype(jnp.float32) + b.astype(jnp.float32)
    return y.astype(x.dtype)


def causal_depthwise_conv(x, w, b):
    K, C = w.shape
    y = lax.conv_general_dilated(
        x, w[:, None, :].astype(x.dtype), window_strides=(1,), padding=[(K - 1, 0)],
        dimension_numbers=("NWC", "WIO", "NWC"), feature_group_count=C)
    return y + b.astype(x.dtype)


def apply_rotary(t, inv_freq):
    half = inv_freq.shape[0]
    S = t.shape[1]
    ang = jnp.arange(S, dtype=jnp.float32)[:, None] * inv_freq[None, :]
    cos = jnp.cos(ang)[None, :, None, :]
    sin = jnp.sin(ang)[None, :, None, :]
    tf = t.astype(jnp.float32)
    t1, t2, rest = tf[..., :half], tf[..., half:2 * half], tf[..., 2 * half:]
    out = jnp.concatenate([t1 * cos - t2 * sin, t2 * cos + t1 * sin, rest], axis=-1)
    return out.astype(t.dtype)


def swa_sink_attention(q, k, v, sinks):
    Bsz, S, Hq, Dh = q.shape
    Hkv = k.shape[2]
    G = Hq // Hkv
    W = A_WINDOW
    nb = S // W
    qb = q.reshape(Bsz, nb, W, Hkv, G, Dh)

    def with_prev(t):
        tb = t.reshape(Bsz, nb, W, Hkv, Dh)
        prev = jnp.pad(tb, ((0, 0), (1, 0), (0, 0), (0, 0), (0, 0)))[:, :-1]
        return jnp.concatenate([prev, tb], axis=2)

    kk, vv = with_prev(k), with_prev(v)
    s = jnp.einsum("bnqhgd,bnkhd->bnhgqk", qb, kk).astype(jnp.float32) * (Dh ** -0.5)
    i = jnp.arange(W)[:, None]
    j = jnp.arange(2 * W)[None, :]
    n = jnp.arange(nb)[:, None, None]
    valid = (j > i) & (j <= i + W) & (n * W + j >= W)
    s = jnp.where(valid[None, :, None, None], s, -jnp.inf)
    sink = sinks.astype(jnp.float32).reshape(Hkv, G)[None, None, :, :, None, None]
    m = jnp.maximum(jnp.max(s, axis=-1, keepdims=True), sink)
    p = jnp.exp(s - m)
    denom = jnp.sum(p, axis=-1, keepdims=True) + jnp.exp(sink - m)
    o = jnp.einsum("bnhgqk,bnkhd->bnqhgd", (p / denom).astype(v.dtype), vv)
    return o.reshape(Bsz, S, Hq * Dh)


def conformer_conv_branch(u, dw_w, dw_b, cn_g, cn_b):
    a, gate = jnp.split(u, 2, axis=-1)
    h = a * jax.nn.sigmoid(gate)
    h = causal_depthwise_conv(h, dw_w, dw_b)
    h = layer_norm(h, cn_g, cn_b)
    return jax.nn.silu(h)


def even_mixer(x, w_in, sinks, dw_w, dw_b, cn_g, cn_b, w_out):
    Bsz, S, _ = x.shape
    q, k, v, u = _split(x @ w_in, EVEN_SPLITS)
    half = A_ROPE_DIM // 2
    inv = jnp.power(jnp.float32(ROPE_THETA), -jnp.arange(half, dtype=jnp.float32) / half)
    q = apply_rotary(q.reshape(Bsz, S, A_HEADS, A_HEAD_DIM), inv)
    k = apply_rotary(k.reshape(Bsz, S, A_KV_HEADS, A_HEAD_DIM), inv)
    v = v.reshape(Bsz, S, A_KV_HEADS, A_HEAD_DIM)
    y_attn = swa_sink_attention(q, k, v, sinks).astype(x.dtype)
    y_conv = conformer_conv_branch(u, dw_w, dw_b, cn_g, cn_b).astype(x.dtype)
    return jnp.concatenate([y_attn, y_conv], axis=-1) @ w_out


def ssd_chunked(x, dt, a, bm, cm):
    Bsz, S, _ = x.shape
    L = C_CHUNK
    nc = S // L
    Hg = C_HEADS // C_GROUPS
    f32 = jnp.float32
    xh = x.astype(f32).reshape(Bsz, nc, L, C_GROUPS, Hg, C_HEAD_DIM)
    dtc = dt.reshape(Bsz, nc, L, C_GROUPS, Hg)
    bc = bm.astype(f32).reshape(Bsz, nc, L, C_GROUPS, C_STATE)
    cc = cm.astype(f32).reshape(Bsz, nc, L, C_GROUPS, C_STATE)
    acs = jnp.cumsum(dtc * a.reshape(C_GROUPS, Hg), axis=2)
    xdt = xh * dtc[..., None]
    seg = acs[:, :, :, None] - acs[:, :, None, :]
    causal = jnp.tril(jnp.ones((L, L), dtype=bool))
    decay = jnp.exp(jnp.where(causal[None, None, :, :, None, None], seg, -jnp.inf))
    cb = jnp.einsum("bclgn,bcsgn->bclsg", cc, bc)
    y_diag = jnp.einsum("bclsgh,bcsghp->bclghp", cb[..., None] * decay, xdt)
    to_end = jnp.exp(acs[:, :, -1:] - acs)
    states = jnp.einsum("bclgn,bclghp->bcghpn", bc, xdt * to_end[..., None])
    chunk_decay = jnp.exp(acs[:, :, -1])

    def step(carry, inp):
        st, dec = inp
        return carry * dec[..., None, None] + st, carry

    init = jnp.zeros((Bsz, C_GROUPS, Hg, C_HEAD_DIM, C_STATE), f32)
    _, prev = lax.scan(step, init, (jnp.moveaxis(states, 1, 0), jnp.moveaxis(chunk_decay, 1, 0)))
    prev = jnp.moveaxis(prev, 0, 1)
    y_off = jnp.einsum("bclgn,bcghpn->bclghp", cc, prev) * jnp.exp(acs)[..., None]
    return (y_diag + y_off).reshape(Bsz, S, C_INNER)


def mamba2_branch(z, xbc, dt, conv_w, conv_b, dt_bias, a_log, d_skip, norm_g):
    Bsz, S, _ = z.shape
    f32 = jnp.float32
    xbc = jax.nn.silu(causal_depthwise_conv(xbc, conv_w, conv_b))
    xs, bm, cm = _split(xbc, (C_INNER, C_GROUPS * C_STATE, C_GROUPS * C_STATE))
    dt = jax.nn.softplus(dt.astype(f32) + dt_bias.astype(f32))
    a = -jnp.exp(a_log.astype(f32))
    y = ssd_chunked(xs, dt, a, bm, cm)
    y = y + xs.astype(f32) * jnp.repeat(d_skip.astype(f32), C_HEAD_DIM)
    y = y * jax.nn.silu(z.astype(f32))
    yg = y.reshape(Bsz, S, C_GROUPS, C_INNER // C_GROUPS)
    yg = yg * lax.rsqrt(jnp.mean(yg * yg, axis=-1, keepdims=True) + NORM_EPS)
    return (yg.reshape(Bsz, S, C_INNER) * norm_g.astype(f32)).astype(z.dtype)


def retention_branch(q, k, v, g, gn_g, gn_b):
    Bsz, S, _ = q.shape
    f32 = jnp.float32
    L = D_CHUNK
    nc = S // L
    inv = 1.0 / jnp.power(jnp.float32(RET_THETA), jnp.linspace(0.0, 1.0, D_QK_DIM // 2, dtype=f32))
    q = apply_rotary(q.reshape(Bsz, S, D_HEADS, D_QK_DIM), inv)
    k = apply_rotary(k.reshape(Bsz, S, D_HEADS, D_QK_DIM), inv)
    qc = q.astype(f32).reshape(Bsz, nc, L, D_HEADS, D_QK_DIM)
    kc = k.astype(f32).reshape(Bsz, nc, L, D_HEADS, D_QK_DIM) * (D_QK_DIM ** -0.5)
    vc = v.astype(f32).reshape(Bsz, nc, L, D_HEADS, D_V_DIM)
    log_gamma = jnp.log(1.0 - jnp.power(2.0, -5.0 - jnp.arange(D_HEADS, dtype=f32)))
    idx = jnp.arange(L, dtype=f32)
    rel = idx[:, None] - idx[None, :]
    dmat = jnp.where(rel[None] >= 0, jnp.exp(log_gamma[:, None, None] * jnp.maximum(rel, 0.0)[None]), 0.0)
    scores = jnp.einsum("bclhd,bcshd->bchls", qc, kc) * dmat
    y = jnp.einsum("bchls,bcshv->bclhv", scores, vc)
    to_end = jnp.exp(log_gamma[None, :] * (L - 1 - idx)[:, None])
    kv = jnp.einsum("bclhd,bclhv->bchdv", kc * to_end[..., None], vc)
    chunk_decay = jnp.exp(log_gamma * L)

    def step(r, kv_c):
        return r * chunk_decay[:, None, None] + kv_c, r

    init = jnp.zeros((Bsz, D_HEADS, D_QK_DIM, D_V_DIM), f32)
    _, prev = lax.scan(step, init, jnp.moveaxis(kv, 1, 0))
    prev = jnp.moveaxis(prev, 0, 1)
    from_start = jnp.exp(log_gamma[None, :] * (idx + 1.0)[:, None])
    y = y + jnp.einsum("bclhd,bchdv->bclhv", qc, prev) * from_start[..., None]
    y = y.reshape(Bsz, S, D_HEADS, D_V_DIM)
    mu = jnp.mean(y, axis=-1, keepdims=True)
    var = jnp.mean(jnp.square(y - mu), axis=-1, keepdims=True)
    y = ((y - mu) * lax.rsqrt(var + NORM_EPS)).reshape(Bsz, S, D_HEADS * D_V_DIM)
    y = y * gn_g.astype(f32) + gn_b.astype(f32)
    return (jax.nn.silu(g.astype(f32)) * y).astype(g.dtype)


def odd_mixer(x, w_in, conv_w, conv_b, dt_bias, a_log, d_skip, ssm_norm_g, ret_gn_g, ret_gn_b, w_out):
    z, xbc, dt, rq, rk, rv, rg = _split(x @ w_in, ODD_SPLITS)
    y_ssm = mamba2_branch(z, xbc, dt, conv_w, conv_b, dt_bias, a_log, d_skip, ssm_norm_g)
    y_ret = retention_branch(rq, rk, rv, rg, ret_gn_g, ret_gn_b)
    return jnp.concatenate([y_ssm, y_ret], axis=-1) @ w_out


def swiglu(x, w_gate, w_up, w_down):
    return (jax.nn.silu(x @ w_gate) * (x @ w_up)) @ w_down


def setup_inputs(seed: int = 0) -> dict:
    key = jax.random.key(seed)
    ks = jax.random.split(key, 25)
    f32 = jnp.float32

    def nrm(k, shape, scale):
        return scale * jax.random.normal(k, shape, f32)

    dt0 = jnp.exp(jax.random.uniform(ks[18], (N_ODD, C_HEADS), f32, math.log(1e-3), math.log(1e-1)))
    return {
        "x": jax.random.normal(ks[0], (BATCH, SEQ, D_MODEL), f32),
        "ln_mix_g": 1.0 + nrm(ks[1], (DEPTH, D_MODEL), 0.02),
        "ln_mix_b": nrm(ks[2], (DEPTH, D_MODEL), 0.02),
        "ln_ffn_g": 1.0 + nrm(ks[3], (DEPTH, D_MODEL), 0.02),
        "ln_ffn_b": nrm(ks[4], (DEPTH, D_MODEL), 0.02),
        "ffn_w_gate": nrm(ks[5], (DEPTH, D_MODEL, FFN_HIDDEN), D_MODEL ** -0.5),
        "ffn_w_up": nrm(ks[6], (DEPTH, D_MODEL, FFN_HIDDEN), D_MODEL ** -0.5),
        "ffn_w_down": nrm(ks[7], (DEPTH, FFN_HIDDEN, D_MODEL), BETA * FFN_HIDDEN ** -0.5),
        "ev_w_in": nrm(ks[8], (N_EVEN, D_MODEL, EVEN_IN), D_MODEL ** -0.5),
        "ev_sinks": nrm(ks[9], (N_EVEN, A_HEADS), 0.5),
        "ev_dw_w": nrm(ks[10], (N_EVEN, B_CONV, B_CHANNELS), B_CONV ** -0.5),
        "ev_dw_b": nrm(ks[11], (N_EVEN, B_CHANNELS), 0.02),
        "ev_cn_g": 1.0 + nrm(ks[12], (N_EVEN, B_CHANNELS), 0.02),
        "ev_cn_b": nrm(ks[13], (N_EVEN, B_CHANNELS), 0.02),
        "ev_w_out": nrm(ks[14], (N_EVEN, EVEN_MIX, D_MODEL), BETA * EVEN_MIX ** -0.5),
        "od_w_in": nrm(ks[15], (N_ODD, D_MODEL, ODD_IN), D_MODEL ** -0.5),
        "od_conv_w": nrm(ks[16], (N_ODD, C_CONV, C_XBC), C_CONV ** -0.5),
        "od_conv_b": nrm(ks[17], (N_ODD, C_XBC), 0.02),
        "od_dt_bias": dt0 + jnp.log(-jnp.expm1(-dt0)),
        "od_a_log": jnp.log(jax.random.uniform(ks[19], (N_ODD, C_HEADS), f32, 1.0, 16.0)),
        "od_d_skip": 1.0 + nrm(ks[20], (N_ODD, C_HEADS), 0.1),
        "od_ssm_norm_g": 1.0 + nrm(ks[21], (N_ODD, C_INNER), 0.02),
        "od_ret_gn_g": 1.0 + nrm(ks[22], (N_ODD, D_HEADS * D_V_DIM), 0.02),
        "od_ret_gn_b": nrm(ks[23], (N_ODD, D_HEADS * D_V_DIM), 0.02),
        "od_w_out": nrm(ks[24], (N_ODD, ODD_MIX, D_MODEL), BETA * ODD_MIX ** -0.5),
    }


def reference(x, ln_mix_g, ln_mix_b, ln_ffn_g, ln_ffn_b, ffn_w_gate, ffn_w_up, ffn_w_down,
              ev_w_in, ev_sinks, ev_dw_w, ev_dw_b, ev_cn_g, ev_cn_b, ev_w_out,
              od_w_in, od_conv_w, od_conv_b, od_dt_bias, od_a_log, od_d_skip, od_ssm_norm_g,
              od_ret_gn_g, od_ret_gn_b, od_w_out):
    for layer in range(DEPTH):
        if layer % 2 == 0:
            e = layer // 2
            h = even_mixer(x, ev_w_in[e], ev_sinks[e], ev_dw_w[e], ev_dw_b[e],
                           ev_cn_g[e], ev_cn_b[e], ev_w_out[e])
        else:
            o = layer // 2
            h = odd_mixer(x, od_w_in[o], od_conv_w[o], od_conv_b[o], od_dt_bias[o], od_a_log[o],
                          od_d_skip[o], od_ssm_norm_g[o], od_ret_gn_g[o], od_ret_gn_b[o], od_w_out[o])
        x = layer_norm(ALPHA * x + h.astype(x.dtype), ln_mix_g[layer], ln_mix_b[layer])
        f = swiglu(x, ffn_w_gate[layer], ffn_w_up[layer], ffn_w_down[layer])
        x = layer_norm(ALPHA * x + f.astype(x.dtype), ln_ffn_g[layer], ln_ffn_b[layer])
    return x
```

```python
import functools
import math

import jax
import jax.numpy as jnp
from jax import lax
from jax.experimental import pallas as pl
from jax.experimental.pallas import tpu as pltpu

F32 = jnp.float32
BF16 = jnp.bfloat16

A_HEADS = 8
A_KV_HEADS = 2
A_HEAD_DIM = 64
A_WINDOW = 128
ROPE_THETA = 500000.0
A_ROPE_DIM = A_HEAD_DIM // 4
B_CONV = 31
C_HEADS = 16
C_HEAD_DIM = 64
C_GROUPS = 2
C_STATE = 128
C_CONV = 4
C_CHUNK = 128
D_HEADS = 4
D_QK_DIM = 128
D_V_DIM = 256
D_CHUNK = 128
RET_THETA = 10000.0
NORM_EPS = 1e-5

LANES = 128
SUBLANES = 8
VMEM_LIMIT = 56 * 1024 * 1024

ROW_TILE = 512
CONV_TILE = 256
CONV_ROWS = 32
CONV_HALO = 32
FFN_CHUNK = 256


def _cparams(*sem):
    return pltpu.CompilerParams(dimension_semantics=sem, vmem_limit_bytes=VMEM_LIMIT)


def _resident(shape):
    nd = len(shape)
    return pl.BlockSpec(shape, lambda *_: (0,) * nd, pipeline_mode=pl.Buffered(1))


def _layer_norm(y, g, b):
    mu = jnp.mean(y, axis=-1, keepdims=True)
    d = y - mu
    var = jnp.mean(d * d, axis=-1, keepdims=True)
    return d * lax.rsqrt(var + NORM_EPS) * g + b


def _silu(v):
    return v * jax.nn.sigmoid(v)


def _dot(a, b):
    return jnp.dot(a, b, preferred_element_type=F32)


def _dot_nt(a, b):
    return lax.dot_general(a, b, (((1,), (1,)), ((), ())), preferred_element_type=F32)


def _dot_tn(a, b):
    return lax.dot_general(a, b, (((0,), (0,)), ((), ())), preferred_element_type=F32)


def _split_bf16(v, parts):
    out = []
    r = v
    for _ in range(parts):
        p = r.astype(BF16)
        out.append(p)
        r = r - p.astype(F32)
    return out


def _dot_exact01(v, m01, parts=3):
    acc = None
    for p in _split_bf16(v, parts):
        t = _dot(p, m01)
        acc = t if acc is None else acc + t
    return acc


def _even_in_kernel(x_ref, wq_ref, wk_ref, wv_ref, wu_ref, cos_ref, sa_ref, sb_ref,
                    q_ref, k_ref, v_ref, h_ref):
    xb = x_ref[...].astype(BF16)
    c, sa, sb = cos_ref[...], sa_ref[...], sb_ref[...]
    rot_dim = A_ROPE_DIM // 2

    def rot(t):
        return t * c + pltpu.roll(t, rot_dim, 1) * sa + pltpu.roll(t, LANES - rot_dim, 1) * sb

    q = _dot(xb, wq_ref[...])
    for j in range(q.shape[1] // LANES):
        q_ref[:, j * LANES:(j + 1) * LANES] = rot(q[:, j * LANES:(j + 1) * LANES]).astype(BF16)
    k = _dot(xb, wk_ref[...])
    for j in range(k.shape[1] // LANES):
        k_ref[:, j * LANES:(j + 1) * LANES] = rot(k[:, j * LANES:(j + 1) * LANES]).astype(BF16)
    v_ref[...] = _dot(xb, wv_ref[...]).astype(BF16)
    u = _dot(xb, wu_ref[...])
    nb = u.shape[1] // 2
    h_ref[...] = u[:, :nb] * jax.nn.sigmoid(u[:, nb:])


def _even_in(x2, wq, wk, wv, wu, cos_t, sa_t, sb_t, seq):
    T, D = x2.shape
    tm = min(ROW_TILE, seq)
    nseq = seq // tm
    row = lambda i: (i, 0)
    tab = pl.BlockSpec((tm, LANES), lambda i: (i % nseq, 0))
    return pl.pallas_call(
        _even_in_kernel,
        grid=(T // tm,),
        in_specs=[pl.BlockSpec((tm, D), row), _resident(wq.shape), _resident(wk.shape),
                  _resident(wv.shape), _resident(wu.shape), tab, tab, tab],
        out_specs=[pl.BlockSpec((tm, wq.shape[1]), row), pl.BlockSpec((tm, wk.shape[1]), row),
                   pl.BlockSpec((tm, wv.shape[1]), row), pl.BlockSpec((tm, wu.shape[1] // 2), row)],
        out_shape=[jax.ShapeDtypeStruct((T, wq.shape[1]), BF16), jax.ShapeDtypeStruct((T, wk.shape[1]), BF16),
                   jax.ShapeDtypeStruct((T, wv.shape[1]), BF16), jax.ShapeDtypeStruct((T, wu.shape[1] // 2), F32)],
        compiler_params=_cparams("parallel"),
        name="even_in",
    )(x2, wq, wk, wv, wu, cos_t, sa_t, sb_t)


def _attn_kernel(sink_ref, q_ref, kp_ref, kc_ref, vp_ref, vc_ref, o_ref):
    n = pl.program_id(1)
    W = A_WINDOW
    G = A_HEADS // A_KV_HEADS
    scale = A_HEAD_DIM ** -0.5
    row = lax.broadcasted_iota(jnp.int32, (W, W), 0)
    col = lax.broadcasted_iota(jnp.int32, (W, W), 1)
    prev_ok = jnp.logical_and(col > row, n > 0)
    cur_ok = col <= row
    lane = lax.broadcasted_iota(jnp.int32, (W, LANES), 1)
    lo = lane < A_HEAD_DIM
    neg = -jnp.inf
    for j in range(A_HEADS // 2):
        h = (2 * j) // G
        q2 = q_ref[:, j * LANES:(j + 1) * LANES]
        kp = kp_ref[:, h * LANES:(h + 1) * LANES]
        kc = kc_ref[:, h * LANES:(h + 1) * LANES]
        acc = None
        for half in range(2):
            head = 2 * j + half
            qm = jnp.where(lo if half == 0 else jnp.logical_not(lo), q2, jnp.zeros_like(q2))
            sp = jnp.where(prev_ok, _dot_nt(qm, kp) * scale, neg)
            sc = jnp.where(cur_ok, _dot_nt(qm, kc) * scale, neg)
            sink = sink_ref[head]
            m = jnp.maximum(jnp.maximum(jnp.max(sp, axis=-1, keepdims=True),
                                        jnp.max(sc, axis=-1, keepdims=True)), sink)
            pp = jnp.exp(sp - m)
            pc = jnp.exp(sc - m)
            denom = (jnp.sum(pp, axis=-1, keepdims=True) + jnp.sum(pc, axis=-1, keepdims=True)
                     + jnp.exp(sink - m))
            inv = 1.0 / denom
            vb = (2 * h + half) * LANES
            t = (_dot((pp * inv).astype(BF16), vp_ref[:, vb:vb + LANES])
                 + _dot((pc * inv).astype(BF16), vc_ref[:, vb:vb + LANES]))
            acc = t if acc is None else acc + t
        o_ref[:, j * LANES:(j + 1) * LANES] = acc.astype(BF16)


def _attention(sinks, q, k, v, batch, seq):
    W = A_WINDOW
    nb = seq // W
    cur = lambda b, n: (b * nb + n, 0)
    prev = lambda b, n: (b * nb + jnp.maximum(n - 1, 0), 0)
    return pl.pallas_call(
        _attn_kernel,
        grid=(batch, nb),
        in_specs=[pl.BlockSpec(memory_space=pltpu.SMEM),
                  pl.BlockSpec((W, q.shape[1]), cur),
                  pl.BlockSpec((W, k.shape[1]), prev), pl.BlockSpec((W, k.shape[1]), cur),
                  pl.BlockSpec((W, v.shape[1]), prev), pl.BlockSpec((W, v.shape[1]), cur)],
        out_specs=pl.BlockSpec((W, q.shape[1]), cur),
        out_shape=jax.ShapeDtypeStruct(q.shape, BF16),
        compiler_params=_cparams("parallel", "parallel"),
        name="swa_attention",
    )(sinks, q, k, k, v, v)


def _conv_kernel(hp_ref, hc_ref, w_ref, b_ref, g_ref, beta_ref, o_ref, buf_ref):
    n = pl.program_id(1)
    tc = hc_ref.shape[0]
    buf_ref[0:CONV_HALO, :] = jnp.where(n > 0, hp_ref[...], 0.0)
    buf_ref[CONV_HALO:CONV_HALO + tc, :] = hc_ref[...]
    first = CONV_HALO - (B_CONV - 1)
    for r0 in range(0, tc, CONV_ROWS):
        acc = jnp.broadcast_to(b_ref[...], (CONV_ROWS, hc_ref.shape[1]))
        for kk in range(B_CONV):
            s = r0 + first + kk
            acc = acc + w_ref[kk:kk + 1, :] * buf_ref[s:s + CONV_ROWS, :]
        y = _layer_norm(acc, g_ref[...], beta_ref[...])
        o_ref[r0:r0 + CONV_ROWS, :] = _silu(y).astype(BF16)


def _conv_module(h, w, b, g, beta, batch, seq):
    T, C = h.shape
    tc = min(CONV_TILE, seq)
    nt = seq // tc
    per = tc // CONV_HALO
    cur = lambda bi, n: (bi * nt + n, 0)
    prev = lambda bi, n: (jnp.maximum((bi * nt + n) * per - 1, 0), 0)
    return pl.pallas_call(
        _conv_kernel,
        grid=(batch, nt),
        in_specs=[pl.BlockSpec((CONV_HALO, C), prev), pl.BlockSpec((tc, C), cur),
                  _resident(w.shape), _resident(b.shape), _resident(g.shape), _resident(beta.shape)],
        out_specs=pl.BlockSpec((tc, C), cur),
        out_shape=jax.ShapeDtypeStruct((T, C), BF16),
        scratch_shapes=[pltpu.VMEM((CONV_HALO + tc, C), F32)],
        compiler_params=_cparams("parallel", "parallel"),
        name="conv_module",
    )(h, h, w, b, g, beta)


def _out_ln_kernel(alpha, ya_ref, yb_ref, x_ref, wa_ref, wb_ref, g_ref, b_ref, o_ref):
    acc = _dot(ya_ref[...], wa_ref[...]) + _dot(yb_ref[...], wb_ref[...])
    o_ref[...] = _layer_norm(alpha * x_ref[...] + acc, g_ref[...], b_ref[...])


def _out_ln(ya, yb, x2, wa, wb, g, b, alpha):
    T, D = x2.shape
    tm = min(ROW_TILE, T)
    row = lambda i: (i, 0)
    return pl.pallas_call(
        functools.partial(_out_ln_kernel, alpha),
        grid=(T // tm,),
        in_specs=[pl.BlockSpec((tm, ya.shape[1]), row), pl.BlockSpec((tm, yb.shape[1]), row),
                  pl.BlockSpec((tm, D), row), _resident(wa.shape), _resident(wb.shape),
                  _resident(g.shape), _resident(b.shape)],
        out_specs=pl.BlockSpec((tm, D), row),
        out_shape=jax.ShapeDtypeStruct((T, D), F32),
        compiler_params=_cparams("parallel"),
        name="out_proj_ln",
    )(ya, yb, x2, wa, wb, g, b)


def _ffn_kernel(alpha, x_ref, wg_ref, wu_ref, wd_ref, g_ref, b_ref, o_ref):
    x = x_ref[...]
    xb = x.astype(BF16)
    hidden = wg_ref.shape[1]
    acc = None
    for c0 in range(0, hidden, FFN_CHUNK):
        gate = _dot(xb, wg_ref[:, c0:c0 + FFN_CHUNK])
        up = _dot(xb, wu_ref[:, c0:c0 + FFN_CHUNK])
        t = _dot((_silu(gate) * up).astype(BF16), wd_ref[c0:c0 + FFN_CHUNK, :])
        acc = t if acc is None else acc + t
    o_ref[...] = _layer_norm(alpha * x + acc, g_ref[...], b_ref[...])


def _ffn(x2, wg, wu, wd, g, b, alpha):
    T, D = x2.shape
    tm = min(ROW_TILE, T)
    row = lambda i: (i, 0)
    return pl.pallas_call(
        functools.partial(_ffn_kernel, alpha),
        grid=(T // tm,),
        in_specs=[pl.BlockSpec((tm, D), row), _resident(wg.shape), _resident(wu.shape),
                  _resident(wd.shape), _resident(g.shape), _resident(b.shape)],
        out_specs=pl.BlockSpec((tm, D), row),
        out_shape=jax.ShapeDtypeStruct((T, D), F32),
        compiler_params=_cparams("parallel"),
        name="ffn_ln",
    )(x2, wg, wu, wd, g, b)


def _odd_in_kernel(x_ref, wz_ref, wx_ref, wdt_ref, wq_ref, wk_ref, wv_ref, wg_ref, cos_ref, sin_ref,
                   z_ref, xbc_ref, dt_ref, q_ref, k_ref, v_ref, g_ref):
    xb = x_ref[...].astype(BF16)
    z_ref[...] = _dot(xb, wz_ref[...])
    xbc_ref[...] = _dot(xb, wx_ref[...])
    dt_ref[...] = _dot(xb, wdt_ref[...])
    c, s = cos_ref[...], sin_ref[...]
    kscale = D_QK_DIM ** -0.5

    def rot(t):
        return t * c + pltpu.roll(t, D_QK_DIM // 2, 1) * s

    q = _dot(xb, wq_ref[...])
    k = _dot(xb, wk_ref[...])
    for j in range(D_HEADS):
        sl = slice(j * D_QK_DIM, (j + 1) * D_QK_DIM)
        q_ref[:, sl] = rot(q[:, sl]).astype(BF16)
        k_ref[:, sl] = rot(k[:, sl]) * kscale
    v_ref[...] = _dot(xb, wv_ref[...]).astype(BF16)
    g_ref[...] = _dot(xb, wg_ref[...])


def _odd_in(x2, wz, wx, wdt, wq, wk, wv, wg, cos_t, sin_t, seq):
    T, D = x2.shape
    tm = min(ROW_TILE // 2, seq)
    nseq = seq // tm
    row = lambda i: (i, 0)
    tab = pl.BlockSpec((tm, LANES), lambda i: (i % nseq, 0))
    ws = (wz, wx, wdt, wq, wk, wv, wg)
    dts = (F32, F32, F32, BF16, F32, BF16, F32)
    return pl.pallas_call(
        _odd_in_kernel,
        grid=(T // tm,),
        in_specs=[pl.BlockSpec((tm, D), row)] + [_resident(w.shape) for w in ws] + [tab, tab],
        out_specs=[pl.BlockSpec((tm, w.shape[1]), row) for w in ws],
        out_shape=[jax.ShapeDtypeStruct((T, w.shape[1]), dt) for w, dt in zip(ws, dts)],
        compiler_params=_cparams("parallel"),
        name="odd_in",
    )(x2, *ws, cos_t, sin_t)


def _ssd_kernel(z_ref, xbc_ref, dtr_ref, cw_ref, cb_ref, dtb_ref, alog_ref, dskip_ref, ng_ref,
                expand_ref, tri_ref, o_ref, buf_ref, tail_ref, state_ref, y_ref):
    c = pl.program_id(1)
    L = C_CHUNK
    inner = C_HEADS * C_HEAD_DIM
    gw = C_STATE
    hpg = C_HEADS // C_GROUPS
    gcols = hpg * C_HEAD_DIM

    @pl.when(c == 0)
    def _():
        tail_ref[...] = jnp.zeros_like(tail_ref)
        state_ref[...] = jnp.zeros_like(state_ref)

    buf_ref[0:SUBLANES, :] = tail_ref[...]
    buf_ref[SUBLANES:SUBLANES + L, :] = xbc_ref[...]
    tail_ref[...] = xbc_ref[L - SUBLANES:L, :]
    first = SUBLANES - (C_CONV - 1)
    conv = jnp.broadcast_to(cb_ref[...], (L, xbc_ref.shape[1]))
    for kk in range(C_CONV):
        conv = conv + cw_ref[kk:kk + 1, :] * buf_ref[first + kk:first + kk + L, :]
    conv = _silu(conv)
    xs = conv[:, :inner]

    dtv = dtr_ref[...] + dtb_ref[...]
    dt = jnp.maximum(dtv, 0.0) + jnp.log1p(jnp.exp(-jnp.abs(dtv)))
    a = -jnp.exp(alog_ref[...])
    acs = _cumsum_rows(dt * a, tri_ref[...])
    acs_t = acs.T
    expand = expand_ref[...]
    acs_e = _dot_exact01(acs, expand)
    dt_e = _dot_exact01(dt, expand)
    last_e = acs_e[L - 1:L, :]
    xdt = xs * dt_e
    wst = (xdt * jnp.exp(last_e - acs_e)).astype(BF16)
    xdt_b = xdt.astype(BF16)
    eacs_e = jnp.exp(acs_e)
    cdec_e = jnp.exp(last_e)

    row = lax.broadcasted_iota(jnp.int32, (L, L), 0)
    col = lax.broadcasted_iota(jnp.int32, (L, L), 1)
    causal = col <= row
    lane = lax.broadcasted_iota(jnp.int32, (L, LANES), 1)
    lo = lane < C_HEAD_DIM
    zero_b = jnp.zeros((L, LANES), BF16)

    z = z_ref[...]
    for g in range(C_GROUPS):
        bm = conv[:, inner + g * gw:inner + (g + 1) * gw].astype(BF16)
        cm = conv[:, inner + C_GROUPS * gw + g * gw:inner + C_GROUPS * gw + (g + 1) * gw].astype(BF16)
        cbm = _dot_nt(cm, bm)
        gs = slice(g * gcols, (g + 1) * gcols)
        prev = state_ref[:, gs]
        y_off = _dot(cm, prev.astype(BF16)) * eacs_e[:, gs]
        new = _dot_tn(bm, wst[:, gs])
        state_ref[:, gs] = prev * cdec_e[:, gs] + new
        for jp in range(hpg // 2):
            cols = slice(g * gcols + jp * LANES, g * gcols + (jp + 1) * LANES)
            xp = xdt_b[:, cols]
            acc = None
            for half in range(2):
                h = g * hpg + 2 * jp + half
                seg = acs[:, h:h + 1] - acs_t[h:h + 1, :]
                dec = jnp.exp(jnp.where(causal, seg, -jnp.inf))
                m = (cbm * dec).astype(BF16)
                xh = jnp.where(lo if half == 0 else jnp.logical_not(lo), xp, zero_b)
                t = _dot(m, xh)
                acc = t if acc is None else acc + t
            lc = slice(jp * LANES, (jp + 1) * LANES)
            y = acc + y_off[:, lc] + xs[:, cols] * dskip_ref[:, cols]
            y_ref[:, cols] = y * _silu(z[:, cols])

    for g in range(C_GROUPS):
        gs = slice(g * gcols, (g + 1) * gcols)
        yg = y_ref[:, gs]
        ms = jnp.mean(yg * yg, axis=-1, keepdims=True)
        o_ref[:, gs] = (yg * lax.rsqrt(ms + NORM_EPS) * ng_ref[:, gs]).astype(BF16)


def _cumsum_rows(v, tri):
    acc = None
    for p in _split_bf16(v, 3):
        t = _dot(tri, p)
        acc = t if acc is None else acc + t
    return acc


def _ssd(z, xbc, dtr, cw, cb, dtb, alog, dskip, ng, expand, tri, batch, seq):
    L = C_CHUNK
    nc = seq // L
    cur = lambda b, c: (b * nc + c, 0)
    consts = (cw, cb, dtb, alog, dskip, ng, expand, tri)
    return pl.pallas_call(
        _ssd_kernel,
        grid=(batch, nc),
        in_specs=[pl.BlockSpec((L, z.shape[1]), cur), pl.BlockSpec((L, xbc.shape[1]), cur),
                  pl.BlockSpec((L, dtr.shape[1]), cur)] + [_resident(t.shape) for t in consts],
        out_specs=pl.BlockSpec((L, z.shape[1]), cur),
        out_shape=jax.ShapeDtypeStruct(z.shape, BF16),
        scratch_shapes=[pltpu.VMEM((SUBLANES + L, xbc.shape[1]), F32),
                        pltpu.VMEM((SUBLANES, xbc.shape[1]), F32),
                        pltpu.VMEM((C_STATE, z.shape[1]), F32),
                        pltpu.VMEM((L, z.shape[1]), F32)],
        compiler_params=_cparams("arbitrary", "arbitrary"),
        name="ssd_scan",
    )(z, xbc, dtr, *consts)


def _ret_kernel(q_ref, k_ref, v_ref, g_ref, dmat_ref, toend_ref, fstart_ref, cdec_ref, gng_ref, gnb_ref,
                o_ref, state_ref):
    c = pl.program_id(1)

    @pl.when(c == 0)
    def _():
        state_ref[...] = jnp.zeros_like(state_ref)

    for h in range(D_HEADS):
        ks = slice(h * D_QK_DIM, (h + 1) * D_QK_DIM)
        vs = slice(h * D_V_DIM, (h + 1) * D_V_DIM)
        qh = q_ref[:, ks]
        kh = k_ref[:, ks]
        vh = v_ref[:, vs]
        scores = _dot_nt(qh, kh.astype(BF16)) * dmat_ref[h]
        prev = state_ref[:, vs]
        y = _dot(scores.astype(BF16), vh) + _dot(qh, prev.astype(BF16)) * fstart_ref[:, vs]
        kd = (kh * toend_ref[:, ks]).astype(BF16)
        state_ref[:, vs] = prev * cdec_ref[:, vs] + _dot_tn(kd, vh)
        mu = jnp.mean(y, axis=-1, keepdims=True)
        d = y - mu
        var = jnp.mean(d * d, axis=-1, keepdims=True)
        yn = d * lax.rsqrt(var + NORM_EPS) * gng_ref[:, vs] + gnb_ref[:, vs]
        o_ref[:, vs] = (_silu(g_ref[:, vs]) * yn).astype(BF16)


def _retention(q, k, v, g, dmat, toend, fstart, cdec, gng, gnb, batch, seq):
    L = D_CHUNK
    nc = seq // L
    cur = lambda b, c: (b * nc + c, 0)
    consts = (dmat, toend, fstart, cdec, gng, gnb)
    return pl.pallas_call(
        _ret_kernel,
        grid=(batch, nc),
        in_specs=[pl.BlockSpec((L, q.shape[1]), cur), pl.BlockSpec((L, k.shape[1]), cur),
                  pl.BlockSpec((L, v.shape[1]), cur), pl.BlockSpec((L, g.shape[1]), cur)]
                 + [_resident(t.shape) for t in consts],
        out_specs=pl.BlockSpec((L, v.shape[1]), cur),
        out_shape=jax.ShapeDtypeStruct(v.shape, BF16),
        scratch_shapes=[pltpu.VMEM((D_QK_DIM, v.shape[1]), F32)],
        compiler_params=_cparams("arbitrary", "arbitrary"),
        name="retention_scan",
    )(q, k, v, g, *consts)


def _even_tables(seq):
    half = A_ROPE_DIM // 2
    inv = jnp.power(jnp.float32(ROPE_THETA), -jnp.arange(half, dtype=F32) / half)
    ang = jnp.arange(seq, dtype=F32)[:, None] * inv[None, :]
    cos, sin = jnp.cos(ang), jnp.sin(ang)
    ones = jnp.ones((seq, A_HEAD_DIM - 2 * half), F32)
    zeros = jnp.zeros((seq, A_HEAD_DIM - 2 * half), F32)
    zh = jnp.zeros((seq, half), F32)
    c = jnp.concatenate([cos, cos, ones], axis=1)
    sa = jnp.concatenate([zh, sin, zeros], axis=1)
    sb = jnp.concatenate([-sin, zh, zeros], axis=1)
    rep = LANES // A_HEAD_DIM
    return jnp.tile(c, (1, rep)), jnp.tile(sa, (1, rep)), jnp.tile(sb, (1, rep))


def _ret_tables(seq):
    half = D_QK_DIM // 2
    inv = 1.0 / jnp.power(jnp.float32(RET_THETA), jnp.linspace(0.0, 1.0, half, dtype=F32))
    ang = jnp.arange(seq, dtype=F32)[:, None] * inv[None, :]
    cos, sin = jnp.cos(ang), jnp.sin(ang)
    L = D_CHUNK
    log_gamma = jnp.log(1.0 - jnp.power(2.0, -5.0 - jnp.arange(D_HEADS, dtype=F32)))
    idx = jnp.arange(L, dtype=F32)
    rel = idx[:, None] - idx[None, :]
    dmat = jnp.where(rel[None] >= 0, jnp.exp(log_gamma[:, None, None] * jnp.maximum(rel, 0.0)[None]), 0.0)
    to_end = jnp.exp(log_gamma[None, :] * (L - 1 - idx)[:, None])
    from_start = jnp.exp(log_gamma[None, :] * (idx + 1.0)[:, None])
    chunk_decay = jnp.exp(log_gamma * L)
    return (jnp.concatenate([cos, cos], axis=1), jnp.concatenate([-sin, sin], axis=1), dmat,
            jnp.repeat(to_end, D_QK_DIM, axis=1), jnp.repeat(from_start, D_V_DIM, axis=1),
            jnp.repeat(chunk_decay, D_V_DIM)[None, :])


def _row(v):
    return v.astype(F32)[None, :]


def _pad_lanes(v, fill=0.0):
    pad = (-v.shape[-1]) % LANES
    return jnp.pad(v, [(0, 0)] * (v.ndim - 1) + [(0, pad)], constant_values=fill)


def _even_mixer(x2, w_in, sinks, dw_w, dw_b, cn_g, cn_b, w_out, ln_g, ln_b, tables, alpha, batch, seq):
    nq = A_HEADS * A_HEAD_DIM
    nkv = A_KV_HEADS * A_HEAD_DIM
    wq = w_in[:, :nq].astype(BF16)
    wk = w_in[:, nq:nq + nkv].astype(BF16)
    wv = w_in[:, nq + nkv:nq + 2 * nkv].astype(BF16)
    wu = w_in[:, nq + 2 * nkv:].astype(BF16)
    D = w_in.shape[0]
    wk = jnp.repeat(wk.reshape(D, A_KV_HEADS, 1, A_HEAD_DIM), 2, axis=2).reshape(D, 2 * nkv)
    wv4 = wv.reshape(D, A_KV_HEADS, 1, A_HEAD_DIM)
    zv = jnp.zeros_like(wv4)
    wv = jnp.concatenate([wv4, zv, zv, wv4], axis=2).reshape(D, 4 * nkv)
    q, k, v, hglu = _even_in(x2, wq, wk, wv, wu, *tables, seq)
    y_attn = _attention(sinks.astype(F32), q, k, v, batch, seq)
    dw = jnp.pad(dw_w.astype(F32), ((0, (-B_CONV) % SUBLANES), (0, 0)))
    y_conv = _conv_module(hglu, dw, _row(dw_b), _row(cn_g), _row(cn_b), batch, seq)
    wo = w_out.astype(BF16)
    return _out_ln(y_attn, y_conv, x2, wo[:nq], wo[nq:], _row(ln_g), _row(ln_b), alpha)


def _odd_mixer(x2, w_in, conv_w, conv_b, dt_bias, a_log, d_skip, ssm_norm_g, gn_g, gn_b, w_out,
               ln_g, ln_b, tables, consts, alpha, batch, seq):
    inner = C_HEADS * C_HEAD_DIM
    xbc_w = inner + 2 * C_GROUPS * C_STATE
    nqk = D_HEADS * D_QK_DIM
    nv = D_HEADS * D_V_DIM
    sizes = (inner, xbc_w, C_HEADS, nqk, nqk, nv, nv)
    offs = [0]
    for s in sizes:
        offs.append(offs[-1] + s)
    wb = w_in.astype(BF16)
    wz, wx, wdt, wq, wk, wv, wg = [wb[:, offs[i]:offs[i + 1]] for i in range(len(sizes))]
    wdt = _pad_lanes(wdt)
    cos_t, sin_t, dmat, toend, fstart, cdec = tables
    expand, tri = consts
    z, xbc, dtr, rq, rk, rv, rg = _odd_in(x2, wz, wx, wdt, wq, wk, wv, wg, cos_t, sin_t, seq)
    y_ssm = _ssd(z, xbc, dtr, conv_w.astype(F32), _row(conv_b), _pad_lanes(_row(dt_bias)),
                 _pad_lanes(_row(a_log)), _row(jnp.repeat(d_skip.astype(F32), C_HEAD_DIM)),
                 _row(ssm_norm_g), expand, tri, batch, seq)
    y_ret = _retention(rq, rk, rv, rg, dmat, toend, fstart, cdec, _row(gn_g), _row(gn_b), batch, seq)
    wo = w_out.astype(BF16)
    return _out_ln(y_ssm, y_ret, x2, wo[:inner], wo[inner:], _row(ln_g), _row(ln_b), alpha)


def _ssd_consts():
    heads = jnp.arange(LANES)[:, None]
    cols = jnp.arange(C_HEADS * C_HEAD_DIM)[None, :] // C_HEAD_DIM
    expand = (heads == cols).astype(BF16)
    i = jnp.arange(C_CHUNK)
    tri = (i[None, :] <= i[:, None]).astype(BF16)
    return expand, tri


def kernel(x, ln_mix_g, ln_mix_b, ln_ffn_g, ln_ffn_b, ffn_w_gate, ffn_w_up, ffn_w_down, ev_w_in, ev_sinks, ev_dw_w, ev_dw_b, ev_cn_g, ev_cn_b, ev_w_out, od_w_in, od_conv_w, od_conv_b, od_dt_bias, od_a_log, od_d_skip, od_ssm_norm_g, od_ret_gn_g, od_ret_gn_b, od_w_out):
    batch, seq, d_model = x.shape
    depth = ln_mix_g.shape[0]
    alpha = float((2 * depth) ** 0.25)
    ev_tables = _even_tables(seq)
    od_tables = _ret_tables(seq)
    od_consts = _ssd_consts()
    x2 = x.reshape(batch * seq, d_model)
    for layer in range(depth):
        i = layer // 2
        if layer % 2 == 0:
            x2 = _even_mixer(x2, ev_w_in[i], ev_sinks[i], ev_dw_w[i], ev_dw_b[i], ev_cn_g[i], ev_cn_b[i],
                             ev_w_out[i], ln_mix_g[layer], ln_mix_b[layer], ev_tables, alpha, batch, seq)
        else:
            x2 = _odd_mixer(x2, od_w_in[i], od_conv_w[i], od_conv_b[i], od_dt_bias[i], od_a_log[i],
                            od_d_skip[i], od_ssm_norm_g[i], od_ret_gn_g[i], od_ret_gn_b[i], od_w_out[i],
                            ln_mix_g[layer], ln_mix_b[layer], od_tables, od_consts, alpha, batch, seq)
        x2 = _ffn(x2, ffn_w_gate[layer].astype(BF16), ffn_w_up[layer].astype(BF16),
                  ffn_w_down[layer].astype(BF16), _row(ln_ffn_g[layer]), _row(ln_ffn_b[layer]), alpha)
    return x2.reshape(batch, seq, d_model)
```

```python
import functools
import math

import jax
import jax.numpy as jnp
from jax import lax
from jax.experimental import pallas as pl
from jax.experimental.pallas import tpu as pltpu

F32 = jnp.float32
BF16 = jnp.bfloat16

A_HEADS = 8
A_KV_HEADS = 2
A_HEAD_DIM = 64
A_WINDOW = 128
ROPE_THETA = 500000.0
A_ROPE_DIM = A_HEAD_DIM // 4
B_CONV = 31
C_HEADS = 16
C_HEAD_DIM = 64
C_GROUPS = 2
C_STATE = 128
C_CONV = 4
C_CHUNK = 128
D_HEADS = 4
D_QK_DIM = 128
D_V_DIM = 256
D_CHUNK = 128
RET_THETA = 10000.0
NORM_EPS = 1e-5

LANES = 128
SUBLANES = 8
VMEM_LIMIT = 56 * 1024 * 1024

ROW_TILE = 512
ATTN_TILE = 512
CONV_TILE = 256
CONV_ROWS = 32
CONV_HALO = 32
FFN_CHUNK = 256
PROJ_COLS = 256
CONV4_ROWS = 256
SCAN_TILE = 256


def _cparams(*sem):
    return pltpu.CompilerParams(dimension_semantics=sem, vmem_limit_bytes=VMEM_LIMIT)


def _resident(shape):
    nd = len(shape)
    return pl.BlockSpec(shape, lambda *_: (0,) * nd, pipeline_mode=pl.Buffered(1))


def _layer_norm(y, g, b):
    mu = jnp.mean(y, axis=-1, keepdims=True)
    d = y - mu
    var = jnp.mean(d * d, axis=-1, keepdims=True)
    return d * lax.rsqrt(var + NORM_EPS) * g + b


def _silu(v):
    return v * jax.nn.sigmoid(v)


def _dot(a, b):
    return jnp.dot(a, b, preferred_element_type=F32)


def _dot_nt(a, b):
    return lax.dot_general(a, b, (((1,), (1,)), ((), ())), preferred_element_type=F32)


def _dot_tn(a, b):
    return lax.dot_general(a, b, (((0,), (0,)), ((), ())), preferred_element_type=F32)


def _split_bf16(v, parts):
    out = []
    r = v
    for _ in range(parts):
        p = r.astype(BF16)
        out.append(p)
        r = r - p.astype(F32)
    return out


def _dot_exact01(v, m01, parts=3):
    acc = None
    for p in _split_bf16(v, parts):
        t = _dot(p, m01)
        acc = t if acc is None else acc + t
    return acc


def _even_in_kernel(x_ref, wq_ref, wk_ref, wv_ref, wu_ref, cos_ref, sa_ref, sb_ref,
                    q_ref, k_ref, v_ref, h_ref):
    xb = x_ref[...].astype(BF16)
    c, sa, sb = cos_ref[...], sa_ref[...], sb_ref[...]
    rot_dim = A_ROPE_DIM // 2

    def rot(t):
        return t * c + pltpu.roll(t, rot_dim, 1) * sa + pltpu.roll(t, LANES - rot_dim, 1) * sb

    qscale = A_HEAD_DIM ** -0.5
    q = _dot(xb, wq_ref[...])
    for j in range(q.shape[1] // LANES):
        q_ref[:, j * LANES:(j + 1) * LANES] = (rot(q[:, j * LANES:(j + 1) * LANES]) * qscale).astype(BF16)
    k = _dot(xb, wk_ref[...])
    for j in range(k.shape[1] // LANES):
        k_ref[:, j * LANES:(j + 1) * LANES] = rot(k[:, j * LANES:(j + 1) * LANES]).astype(BF16)
    v_ref[...] = _dot(xb, wv_ref[...]).astype(BF16)
    u = _dot(xb, wu_ref[...])
    nb = u.shape[1] // 2
    h_ref[...] = u[:, :nb] * jax.nn.sigmoid(u[:, nb:])


def _even_in(x2, wq, wk, wv, wu, cos_t, sa_t, sb_t, seq):
    T, D = x2.shape
    tm = min(ROW_TILE, seq)
    nseq = seq // tm
    row = lambda i: (i, 0)
    tab = pl.BlockSpec((tm, LANES), lambda i: (i % nseq, 0))
    return pl.pallas_call(
        _even_in_kernel,
        grid=(T // tm,),
        in_specs=[pl.BlockSpec((tm, D), row), _resident(wq.shape), _resident(wk.shape),
                  _resident(wv.shape), _resident(wu.shape), tab, tab, tab],
        out_specs=[pl.BlockSpec((tm, wq.shape[1]), row), pl.BlockSpec((tm, wk.shape[1]), row),
                   pl.BlockSpec((tm, wv.shape[1]), row), pl.BlockSpec((tm, wu.shape[1] // 2), row)],
        out_shape=[jax.ShapeDtypeStruct((T, wq.shape[1]), BF16), jax.ShapeDtypeStruct((T, wk.shape[1]), BF16),
                   jax.ShapeDtypeStruct((T, wv.shape[1]), BF16), jax.ShapeDtypeStruct((T, wu.shape[1] // 2), F32)],
        compiler_params=_cparams("parallel"),
        name="even_in",
    )(x2, wq, wk, wv, wu, cos_t, sa_t, sb_t)


def _attn_kernel(sink_ref, q_ref, kp_ref, kc_ref, vp_ref, vc_ref, o_ref):
    n = pl.program_id(1)
    W = A_WINDOW
    G = A_HEADS // A_KV_HEADS
    nsub = q_ref.shape[0] // W
    r = lax.broadcasted_iota(jnp.int32, (G * W, 2 * W), 0)
    i = jnp.bitwise_and(r, W - 1)
    c = lax.broadcasted_iota(jnp.int32, (G * W, 2 * W), 1)
    band = jnp.logical_and(c > i, c <= i + W)
    neg = jnp.float32(-jnp.inf)
    bias_inner = jnp.where(band, 0.0, neg)
    bias_first = jnp.where(jnp.logical_and(band, jnp.logical_or(c >= W, n > 0)), 0.0, neg)
    rg = lax.broadcasted_iota(jnp.int32, (G * W, 1), 0) // W
    lane = lax.broadcasted_iota(jnp.int32, (W, LANES), 1)
    lo = lane < A_HEAD_DIM
    ones = jnp.ones((2 * W, LANES), BF16)
    for sub in range(nsub):
        rows = slice(sub * W, (sub + 1) * W)
        prow = slice((sub - 1) * W, sub * W)
        bias = bias_first if sub == 0 else bias_inner
        for h in range(A_KV_HEADS):
            kl = slice(h * LANES, (h + 1) * LANES)
            kprev = kp_ref[:, kl] if sub == 0 else kc_ref[prow, kl]
            kcat = jnp.concatenate([kprev, kc_ref[rows, kl]], axis=0)
            parts = []
            for jj in range(G // 2):
                j = h * (G // 2) + jj
                q2 = q_ref[rows, j * LANES:(j + 1) * LANES]
                parts.append(jnp.where(lo, q2, jnp.zeros_like(q2)))
                parts.append(jnp.where(lo, jnp.zeros_like(q2), q2))
            s = _dot_nt(jnp.concatenate(parts, axis=0), kcat) + bias
            sink = jnp.full((G * W, 1), sink_ref[h * G], F32)
            for gi in range(1, G):
                sink = jnp.where(rg == gi, sink_ref[h * G + gi], sink)
            m = jnp.maximum(jnp.max(jnp.maximum(s[:, :W], s[:, W:]), axis=-1, keepdims=True), sink)
            p = jnp.exp(s - m).astype(BF16)
            denom = _dot(p, ones) + jnp.exp(sink - m)
            for jj in range(G // 2):
                j = h * (G // 2) + jj
                acc = None
                for half in range(2):
                    vl = slice((2 * h + half) * LANES, (2 * h + half + 1) * LANES)
                    vprev = vp_ref[:, vl] if sub == 0 else vc_ref[prow, vl]
                    vcat = jnp.concatenate([vprev, vc_ref[rows, vl]], axis=0)
                    hr = slice((2 * jj + half) * W, (2 * jj + half + 1) * W)
                    t = _dot(p[hr], vcat)
                    acc = t if acc is None else acc + t
                d = jnp.where(lo, denom[(2 * jj) * W:(2 * jj + 1) * W], denom[(2 * jj + 1) * W:(2 * jj + 2) * W])
                o_ref[rows, j * LANES:(j + 1) * LANES] = (acc * (1.0 / d)).astype(BF16)


def _attention(sinks, q, k, v, batch, seq):
    W = A_WINDOW
    tq = min(ATTN_TILE, seq)
    per = tq // W
    nt = seq // tq
    cur = lambda b, n: (b * nt + n, 0)
    prev = lambda b, n: (jnp.maximum((b * nt + n) * per - 1, 0), 0)
    return pl.pallas_call(
        _attn_kernel,
        grid=(batch, nt),
        in_specs=[pl.BlockSpec(memory_space=pltpu.SMEM),
                  pl.BlockSpec((tq, q.shape[1]), cur),
                  pl.BlockSpec((W, k.shape[1]), prev), pl.BlockSpec((tq, k.shape[1]), cur),
                  pl.BlockSpec((W, v.shape[1]), prev), pl.BlockSpec((tq, v.shape[1]), cur)],
        out_specs=pl.BlockSpec((tq, q.shape[1]), cur),
        out_shape=jax.ShapeDtypeStruct(q.shape, BF16),
        compiler_params=_cparams("parallel", "parallel"),
        name="swa_attention",
    )(sinks, q, k, k, v, v)


def _conv_kernel(hp_ref, hc_ref, w_ref, b_ref, g_ref, beta_ref, o_ref, var_ref):
    n = pl.program_id(1)
    tc, C = hc_ref.shape
    ext_rows = CONV_HALO + tc
    for c0 in range(0, C, LANES):
        cols = slice(c0, c0 + LANES)
        ext = jnp.concatenate([jnp.where(n > 0, hp_ref[:, cols], 0.0), hc_ref[:, cols]], axis=0)
        var_ref[0, :, :, cols] = ext.reshape(ext_rows // SUBLANES, SUBLANES, LANES)
        for r in range(1, SUBLANES):
            sh = pltpu.roll(ext, ext_rows - r, 0)
            var_ref[r, :, :, cols] = sh.reshape(ext_rows // SUBLANES, SUBLANES, LANES)
    first = CONV_HALO - (B_CONV - 1)
    nt = CONV_ROWS // SUBLANES
    for r0 in range(0, tc, CONV_ROWS):
        acc = None
        for kk in range(B_CONV):
            d = first + kk
            s = r0 // SUBLANES + d // SUBLANES
            t = w_ref[kk] * var_ref[d % SUBLANES, s:s + nt]
            acc = t + b_ref[...] if acc is None else acc + t
        y = _layer_norm(acc, g_ref[...], beta_ref[...])
        o_ref[r0:r0 + CONV_ROWS, :] = _silu(y).reshape(CONV_ROWS, C).astype(BF16)


def _conv_module(h, w, b, g, beta, batch, seq):
    T, C = h.shape
    tc = min(CONV_TILE, seq)
    nt = seq // tc
    per = tc // CONV_HALO
    cur = lambda bi, n: (bi * nt + n, 0)
    prev = lambda bi, n: (jnp.maximum((bi * nt + n) * per - 1, 0), 0)
    return pl.pallas_call(
        _conv_kernel,
        grid=(batch, nt),
        in_specs=[pl.BlockSpec((CONV_HALO, C), prev), pl.BlockSpec((tc, C), cur),
                  _resident(w.shape), _resident(b.shape), _resident(g.shape), _resident(beta.shape)],
        out_specs=pl.BlockSpec((tc, C), cur),
        out_shape=jax.ShapeDtypeStruct((T, C), BF16),
        scratch_shapes=[pltpu.VMEM((SUBLANES, (CONV_HALO + tc) // SUBLANES, SUBLANES, C), F32)],
        compiler_params=_cparams("parallel", "parallel"),
        name="conv_module",
    )(h, h, w, b, g, beta)


def _out_ln_kernel(alpha, ya_ref, yb_ref, x_ref, wa_ref, wb_ref, g_ref, b_ref, o_ref):
    acc = _dot(ya_ref[...], wa_ref[...]) + _dot(yb_ref[...], wb_ref[...])
    o_ref[...] = _layer_norm(alpha * x_ref[...] + acc, g_ref[...], b_ref[...])


def _out_ln(ya, yb, x2, wa, wb, g, b, alpha):
    T, D = x2.shape
    tm = min(ROW_TILE, T)
    row = lambda i: (i, 0)
    return pl.pallas_call(
        functools.partial(_out_ln_kernel, alpha),
        grid=(T // tm,),
        in_specs=[pl.BlockSpec((tm, ya.shape[1]), row), pl.BlockSpec((tm, yb.shape[1]), row),
                  pl.BlockSpec((tm, D), row), _resident(wa.shape), _resident(wb.shape),
                  _resident(g.shape), _resident(b.shape)],
        out_specs=pl.BlockSpec((tm, D), row),
        out_shape=jax.ShapeDtypeStruct((T, D), F32),
        compiler_params=_cparams("parallel"),
        name="out_proj_ln",
    )(ya, yb, x2, wa, wb, g, b)


def _ffn_kernel(alpha, x_ref, wg_ref, wu_ref, wd_ref, g_ref, b_ref, o_ref):
    x = x_ref[...]
    xb = x.astype(BF16)
    hidden = wg_ref.shape[1]
    acc = None
    for c0 in range(0, hidden, FFN_CHUNK):
        gate = _dot(xb, wg_ref[:, c0:c0 + FFN_CHUNK])
        up = _dot(xb, wu_ref[:, c0:c0 + FFN_CHUNK])
        t = _dot((_silu(gate) * up).astype(BF16), wd_ref[c0:c0 + FFN_CHUNK, :])
        acc = t if acc is None else acc + t
    o_ref[...] = _layer_norm(alpha * x + acc, g_ref[...], b_ref[...])


def _ffn(x2, wg, wu, wd, g, b, alpha):
    T, D = x2.shape
    tm = min(ROW_TILE, T)
    row = lambda i: (i, 0)
    return pl.pallas_call(
        functools.partial(_ffn_kernel, alpha),
        grid=(T // tm,),
        in_specs=[pl.BlockSpec((tm, D), row), _resident(wg.shape), _resident(wu.shape),
                  _resident(wd.shape), _resident(g.shape), _resident(b.shape)],
        out_specs=pl.BlockSpec((tm, D), row),
        out_shape=jax.ShapeDtypeStruct((T, D), F32),
        compiler_params=_cparams("parallel"),
        name="ffn_ln",
    )(x2, wg, wu, wd, g, b)


def _odd_in_kernel(nseq, x_ref, wz_ref, wx_ref, wdt_ref, wq_ref, wk_ref, wv_ref, wg_ref, cos_ref, sin_ref,
                   cw_ref, cb_ref, dtb_ref,
                   z_ref, xs_ref, bc_ref, dt_ref, q_ref, k_ref, v_ref, g_ref, buf_ref):
    i = pl.program_id(0)
    tm = x_ref.shape[0]
    inner = xs_ref.shape[1]
    xb = x_ref[...].astype(BF16)

    def col_blocks(w_ref):
        for c0 in range(0, w_ref.shape[1], PROJ_COLS):
            c1 = min(c0 + PROJ_COLS, w_ref.shape[1])
            yield c0, c1, _dot(xb, w_ref[:, c0:c1])

    @pl.when(i % nseq == 0)
    def _():
        buf_ref[0:SUBLANES, :] = jnp.zeros((SUBLANES, buf_ref.shape[1]), F32)

    @pl.when(i % nseq != 0)
    def _():
        buf_ref[0:SUBLANES, :] = buf_ref[tm:tm + SUBLANES, :]

    rblk = min(CONV4_ROWS, tm)
    ntile = (SUBLANES + rblk) // SUBLANES

    def conv_block(r0, c0):
        cols = slice(c0, c0 + LANES)
        ext = buf_ref[r0:r0 + SUBLANES + rblk, cols]
        acc = cb_ref[:, cols] + cw_ref[C_CONV - 1, :, cols] * ext.reshape(ntile, SUBLANES, LANES)
        for kk in range(1, C_CONV):
            sh = pltpu.roll(ext, kk, 0).reshape(ntile, SUBLANES, LANES)
            acc = acc + cw_ref[C_CONV - 1 - kk, :, cols] * sh
        y = _silu(acc[1:]).reshape(rblk, LANES)
        if c0 < inner:
            xs_ref[r0:r0 + rblk, cols] = y
        else:
            bc_ref[r0:r0 + rblk, c0 - inner:c0 - inner + LANES] = y.astype(BF16)

    for c0, c1, r in col_blocks(wx_ref):
        buf_ref[SUBLANES:SUBLANES + tm, c0:c1] = r
        for cc in range(c0, c1, LANES):
            for r0 in range(0, tm, rblk):
                conv_block(r0, cc)

    for c0, c1, r in col_blocks(wz_ref):
        z_ref[:, c0:c1] = _silu(r)

    c, s = cos_ref[...], sin_ref[...]
    kscale = D_QK_DIM ** -0.5

    def rot(t):
        return t * c + pltpu.roll(t, D_QK_DIM // 2, 1) * s

    for c0, c1, r in col_blocks(wq_ref):
        for cc in range(c0, c1, D_QK_DIM):
            q_ref[:, cc:cc + D_QK_DIM] = rot(r[:, cc - c0:cc - c0 + D_QK_DIM]).astype(BF16)
    for c0, c1, r in col_blocks(wk_ref):
        for cc in range(c0, c1, D_QK_DIM):
            k_ref[:, cc:cc + D_QK_DIM] = rot(r[:, cc - c0:cc - c0 + D_QK_DIM]) * kscale
    for c0, c1, r in col_blocks(wv_ref):
        v_ref[:, c0:c1] = r.astype(BF16)
    for c0, c1, r in col_blocks(wg_ref):
        g_ref[:, c0:c1] = _silu(r)
    dtv = _dot(xb, wdt_ref[...]) + dtb_ref[...]
    dt_ref[...] = jnp.maximum(dtv, 0.0) + jnp.log1p(jnp.exp(-jnp.abs(dtv)))


def _odd_in(x2, wz, wx, wdt, wq, wk, wv, wg, cos_t, sin_t, cw, cb, dtb, seq):
    T, D = x2.shape
    tm = min(ROW_TILE, seq)
    nseq = seq // tm
    inner = wz.shape[1]
    row = lambda i: (i, 0)
    tab = pl.BlockSpec((tm, LANES), lambda i: (i % nseq, 0))
    ws = (wz, wx, wdt, wq, wk, wv, wg)
    consts = (cw, cb, dtb)
    widths = (inner, inner, wx.shape[1] - inner, wdt.shape[1], wq.shape[1], wk.shape[1], wv.shape[1], wg.shape[1])
    dts = (F32, F32, BF16, F32, BF16, F32, BF16, F32)
    return pl.pallas_call(
        functools.partial(_odd_in_kernel, nseq),
        grid=(T // tm,),
        in_specs=[pl.BlockSpec((tm, D), row)] + [_resident(w.shape) for w in ws] + [tab, tab]
                 + [_resident(t.shape) for t in consts],
        out_specs=[pl.BlockSpec((tm, n), row) for n in widths],
        out_shape=[jax.ShapeDtypeStruct((T, n), dt) for n, dt in zip(widths, dts)],
        scratch_shapes=[pltpu.VMEM((SUBLANES + tm, wx.shape[1]), F32)],
        compiler_params=_cparams("arbitrary"),
        name="odd_in",
    )(x2, *ws, cos_t, sin_t, *consts)


def _ssd_kernel(z_ref, xs_ref, bc_ref, dt_ref, alog_ref, dskip_ref, ng_ref,
                expand_ref, tri_ref, o_ref, state_ref, y_ref):
    @pl.when(pl.program_id(1) == 0)
    def _():
        state_ref[...] = jnp.zeros_like(state_ref)

    L = C_CHUNK
    for r0 in range(0, z_ref.shape[0], L):
        rows = slice(r0, r0 + L)
        _ssd_chunk(z_ref.at[rows], xs_ref.at[rows], bc_ref.at[rows], dt_ref.at[rows], alog_ref, dskip_ref,
                   ng_ref, expand_ref, tri_ref, o_ref.at[rows], state_ref, y_ref)


def _ssd_chunk(z_ref, xs_ref, bc_ref, dt_ref, alog_ref, dskip_ref, ng_ref, expand_ref, tri_ref,
               o_ref, state_ref, y_ref):
    L = C_CHUNK
    gw = C_STATE
    hpg = C_HEADS // C_GROUPS
    gcols = hpg * C_HEAD_DIM
    xs = xs_ref[...]

    dt = dt_ref[...]
    a = -jnp.exp(alog_ref[...])
    acs = _cumsum_rows(dt * a, tri_ref[...])
    acs_t = acs.T
    expand = expand_ref[...]
    acs_e = _dot_exact01(acs, expand)
    dt_e = _dot_exact01(dt, expand)
    last_e = acs_e[L - 1:L, :]
    xdt = xs * dt_e
    wst = (xdt * jnp.exp(last_e - acs_e)).astype(BF16)
    xdt_b = xdt.astype(BF16)
    eacs_e = jnp.exp(acs_e)
    cdec_e = jnp.exp(last_e)

    row = lax.broadcasted_iota(jnp.int32, (L, L), 0)
    col = lax.broadcasted_iota(jnp.int32, (L, L), 1)
    causal = col <= row
    lane = lax.broadcasted_iota(jnp.int32, (L, LANES), 1)
    lo = lane < C_HEAD_DIM
    zero_b = jnp.zeros((L, LANES), BF16)

    for g in range(C_GROUPS):
        bm = bc_ref[:, g * gw:(g + 1) * gw]
        cm = bc_ref[:, (C_GROUPS + g) * gw:(C_GROUPS + g + 1) * gw]
        cbm = _dot_nt(cm, bm)
        gs = slice(g * gcols, (g + 1) * gcols)
        prev = state_ref[:, gs]
        y_off = _dot(cm, prev.astype(BF16)) * eacs_e[:, gs]
        new = _dot_tn(bm, wst[:, gs])
        state_ref[:, gs] = prev * cdec_e[:, gs] + new
        for jp in range(hpg // 2):
            cols = slice(g * gcols + jp * LANES, g * gcols + (jp + 1) * LANES)
            xp = xdt_b[:, cols]
            acc = None
            for half in range(2):
                h = g * hpg + 2 * jp + half
                seg = acs[:, h:h + 1] - acs_t[h:h + 1, :]
                dec = jnp.exp(jnp.where(causal, seg, -jnp.inf))
                m = (cbm * dec).astype(BF16)
                xh = jnp.where(lo if half == 0 else jnp.logical_not(lo), xp, zero_b)
                t = _dot(m, xh)
                acc = t if acc is None else acc + t
            lc = slice(jp * LANES, (jp + 1) * LANES)
            y = acc + y_off[:, lc] + xs[:, cols] * dskip_ref[:, cols]
            y_ref[:, cols] = y * z_ref[:, cols]

    for g in range(C_GROUPS):
        gs = slice(g * gcols, (g + 1) * gcols)
        yg = y_ref[:, gs]
        ms = jnp.mean(yg * yg, axis=-1, keepdims=True)
        o_ref[:, gs] = (yg * lax.rsqrt(ms + NORM_EPS) * ng_ref[:, gs]).astype(BF16)


def _cumsum_rows(v, tri):
    acc = None
    for p in _split_bf16(v, 3):
        t = _dot(tri, p)
        acc = t if acc is None else acc + t
    return acc


def _ssd(z, xs, bc, dt, alog, dskip, ng, expand, tri, batch, seq):
    tr = min(SCAN_TILE, seq)
    nt = seq // tr
    cur = lambda b, c: (b * nt + c, 0)
    consts = (alog, dskip, ng, expand, tri)
    return pl.pallas_call(
        _ssd_kernel,
        grid=(batch, nt),
        in_specs=[pl.BlockSpec((tr, t.shape[1]), cur) for t in (z, xs, bc, dt)]
                 + [_resident(t.shape) for t in consts],
        out_specs=pl.BlockSpec((tr, z.shape[1]), cur),
        out_shape=jax.ShapeDtypeStruct(z.shape, BF16),
        scratch_shapes=[pltpu.VMEM((C_STATE, z.shape[1]), F32),
                        pltpu.VMEM((C_CHUNK, z.shape[1]), F32)],
        compiler_params=_cparams("arbitrary", "arbitrary"),
        name="ssd_scan",
    )(z, xs, bc, dt, *consts)


def _ret_kernel(q_ref, k_ref, v_ref, g_ref, dmat_ref, toend_ref, fstart_ref, cdec_ref, gng_ref, gnb_ref,
                o_ref, state_ref):
    @pl.when(pl.program_id(1) == 0)
    def _():
        state_ref[...] = jnp.zeros_like(state_ref)

    L = D_CHUNK
    for r0 in range(0, q_ref.shape[0], L):
        rows = slice(r0, r0 + L)
        for h in range(D_HEADS):
            ks = slice(h * D_QK_DIM, (h + 1) * D_QK_DIM)
            vs = slice(h * D_V_DIM, (h + 1) * D_V_DIM)
            qh = q_ref[rows, ks]
            kh = k_ref[rows, ks]
            vh = v_ref[rows, vs]
            scores = _dot_nt(qh, kh.astype(BF16)) * dmat_ref[h]
            prev = state_ref[:, vs]
            y = _dot(scores.astype(BF16), vh) + _dot(qh, prev.astype(BF16)) * fstart_ref[:, vs]
            kd = (kh * toend_ref[:, ks]).astype(BF16)
            state_ref[:, vs] = prev * cdec_ref[:, vs] + _dot_tn(kd, vh)
            mu = jnp.mean(y, axis=-1, keepdims=True)
            d = y - mu
            var = jnp.mean(d * d, axis=-1, keepdims=True)
            yn = d * lax.rsqrt(var + NORM_EPS) * gng_ref[:, vs] + gnb_ref[:, vs]
            o_ref[rows, vs] = (g_ref[rows, vs] * yn).astype(BF16)


def _retention(q, k, v, g, dmat, toend, fstart, cdec, gng, gnb, batch, seq):
    L = min(SCAN_TILE, seq)
    nc = seq // L
    cur = lambda b, c: (b * nc + c, 0)
    consts = (dmat, toend, fstart, cdec, gng, gnb)
    return pl.pallas_call(
        _ret_kernel,
        grid=(batch, nc),
        in_specs=[pl.BlockSpec((L, q.shape[1]), cur), pl.BlockSpec((L, k.shape[1]), cur),
                  pl.BlockSpec((L, v.shape[1]), cur), pl.BlockSpec((L, g.shape[1]), cur)]
                 + [_resident(t.shape) for t in consts],
        out_specs=pl.BlockSpec((L, v.shape[1]), cur),
        out_shape=jax.ShapeDtypeStruct(v.shape, BF16),
        scratch_shapes=[pltpu.VMEM((D_QK_DIM, v.shape[1]), F32)],
        compiler_params=_cparams("arbitrary", "arbitrary"),
        name="retention_scan",
    )(q, k, v, g, *consts)


def _even_tables(seq):
    half = A_ROPE_DIM // 2
    inv = jnp.power(jnp.float32(ROPE_THETA), -jnp.arange(half, dtype=F32) / half)
    ang = jnp.arange(seq, dtype=F32)[:, None] * inv[None, :]
    cos, sin = jnp.cos(ang), jnp.sin(ang)
    ones = jnp.ones((seq, A_HEAD_DIM - 2 * half), F32)
    zeros = jnp.zeros((seq, A_HEAD_DIM - 2 * half), F32)
    zh = jnp.zeros((seq, half), F32)
    c = jnp.concatenate([cos, cos, ones], axis=1)
    sa = jnp.concatenate([zh, sin, zeros], axis=1)
    sb = jnp.concatenate([-sin, zh, zeros], axis=1)
    rep = LANES // A_HEAD_DIM
    return jnp.tile(c, (1, rep)), jnp.tile(sa, (1, rep)), jnp.tile(sb, (1, rep))


def _ret_tables(seq):
    half = D_QK_DIM // 2
    inv = 1.0 / jnp.power(jnp.float32(RET_THETA), jnp.linspace(0.0, 1.0, half, dtype=F32))
    ang = jnp.arange(seq, dtype=F32)[:, None] * inv[None, :]
    cos, sin = jnp.cos(ang), jnp.sin(ang)
    L = D_CHUNK
    log_gamma = jnp.log(1.0 - jnp.power(2.0, -5.0 - jnp.arange(D_HEADS, dtype=F32)))
    idx = jnp.arange(L, dtype=F32)
    rel = idx[:, None] - idx[None, :]
    dmat = jnp.where(rel[None] >= 0, jnp.exp(log_gamma[:, None, None] * jnp.maximum(rel, 0.0)[None]), 0.0)
    to_end = jnp.exp(log_gamma[None, :] * (L - 1 - idx)[:, None])
    from_start = jnp.exp(log_gamma[None, :] * (idx + 1.0)[:, None])
    chunk_decay = jnp.exp(log_gamma * L)
    return (jnp.concatenate([cos, cos], axis=1), jnp.concatenate([-sin, sin], axis=1), dmat,
            jnp.repeat(to_end, D_QK_DIM, axis=1), jnp.repeat(from_start, D_V_DIM, axis=1),
            jnp.repeat(chunk_decay, D_V_DIM)[None, :])


def _row(v):
    return v.astype(F32)[None, :]


def _sublane_rep(v):
    v = v.astype(F32)
    return jnp.broadcast_to(v[..., None, :], v.shape[:-1] + (SUBLANES, v.shape[-1]))


def _pad_lanes(v, fill=0.0):
    pad = (-v.shape[-1]) % LANES
    return jnp.pad(v, [(0, 0)] * (v.ndim - 1) + [(0, pad)], constant_values=fill)


def _even_mixer(x2, w_in, sinks, dw_w, dw_b, cn_g, cn_b, w_out, ln_g, ln_b, tables, alpha, batch, seq):
    nq = A_HEADS * A_HEAD_DIM
    nkv = A_KV_HEADS * A_HEAD_DIM
    wq = w_in[:, :nq].astype(BF16)
    wk = w_in[:, nq:nq + nkv].astype(BF16)
    wv = w_in[:, nq + nkv:nq + 2 * nkv].astype(BF16)
    wu = w_in[:, nq + 2 * nkv:].astype(BF16)
    D = w_in.shape[0]
    wk = jnp.repeat(wk.reshape(D, A_KV_HEADS, 1, A_HEAD_DIM), 2, axis=2).reshape(D, 2 * nkv)
    wv4 = wv.reshape(D, A_KV_HEADS, 1, A_HEAD_DIM)
    zv = jnp.zeros_like(wv4)
    wv = jnp.concatenate([wv4, zv, zv, wv4], axis=2).reshape(D, 4 * nkv)
    q, k, v, hglu = _even_in(x2, wq, wk, wv, wu, *tables, seq)
    y_attn = _attention(sinks.astype(F32), q, k, v, batch, seq)
    y_conv = _conv_module(hglu, _sublane_rep(dw_w), _sublane_rep(dw_b), _sublane_rep(cn_g), _sublane_rep(cn_b),
                          batch, seq)
    wo = w_out.astype(BF16)
    return _out_ln(y_attn, y_conv, x2, wo[:nq], wo[nq:], _row(ln_g), _row(ln_b), alpha)


def _odd_mixer(x2, w_in, conv_w, conv_b, dt_bias, a_log, d_skip, ssm_norm_g, gn_g, gn_b, w_out,
               ln_g, ln_b, tables, consts, alpha, batch, seq):
    inner = C_HEADS * C_HEAD_DIM
    xbc_w = inner + 2 * C_GROUPS * C_STATE
    nqk = D_HEADS * D_QK_DIM
    nv = D_HEADS * D_V_DIM
    sizes = (inner, xbc_w, C_HEADS, nqk, nqk, nv, nv)
    offs = [0]
    for s in sizes:
        offs.append(offs[-1] + s)
    wb = w_in.astype(BF16)
    wz, wx, wdt, wq, wk, wv, wg = [wb[:, offs[i]:offs[i + 1]] for i in range(len(sizes))]
    wdt = _pad_lanes(wdt)
    cos_t, sin_t, dmat, toend, fstart, cdec = tables
    expand, tri = consts
    z, xs, bc, dt, rq, rk, rv, rg = _odd_in(x2, wz, wx, wdt, wq, wk, wv, wg, cos_t, sin_t,
                                            _sublane_rep(conv_w), _sublane_rep(conv_b),
                                            _pad_lanes(_row(dt_bias)), seq)
    y_ssm = _ssd(z, xs, bc, dt, _pad_lanes(_row(a_log)), _row(jnp.repeat(d_skip.astype(F32), C_HEAD_DIM)),
                 _row(ssm_norm_g), expand, tri, batch, seq)
    y_ret = _retention(rq, rk, rv, rg, dmat, toend, fstart, cdec, _row(gn_g), _row(gn_b), batch, seq)
    wo = w_out.astype(BF16)
    return _out_ln(y_ssm, y_ret, x2, wo[:inner], wo[inner:], _row(ln_g), _row(ln_b), alpha)


def _ssd_consts():
    heads = jnp.arange(LANES)[:, None]
    cols = jnp.arange(C_HEADS * C_HEAD_DIM)[None, :] // C_HEAD_DIM
    expand = (heads == cols).astype(BF16)
    i = jnp.arange(C_CHUNK)
    tri = (i[None, :] <= i[:, None]).astype(BF16)
    return expand, tri


def kernel(x, ln_mix_g, ln_mix_b, ln_ffn_g, ln_ffn_b, ffn_w_gate, ffn_w_up, ffn_w_down, ev_w_in, ev_sinks, ev_dw_w, ev_dw_b, ev_cn_g, ev_cn_b, ev_w_out, od_w_in, od_conv_w, od_conv_b, od_dt_bias, od_a_log, od_d_skip, od_ssm_norm_g, od_ret_gn_g, od_ret_gn_b, od_w_out):
    batch, seq, d_model = x.shape
    depth = ln_mix_g.shape[0]
    alpha = float((2 * depth) ** 0.25)
    ev_tables = _even_tables(seq)
    od_tables = _ret_tables(seq)
    od_consts = _ssd_consts()
    x2 = x.reshape(batch * seq, d_model)
    for layer in range(depth):
        i = layer // 2
        if layer % 2 == 0:
            x2 = _even_mixer(x2, ev_w_in[i], ev_sinks[i], ev_dw_w[i], ev_dw_b[i], ev_cn_g[i], ev_cn_b[i],
                             ev_w_out[i], ln_mix_g[layer], ln_mix_b[layer], ev_tables, alpha, batch, seq)
        else:
            x2 = _odd_mixer(x2, od_w_in[i], od_conv_w[i], od_conv_b[i], od_dt_bias[i], od_a_log[i],
                            od_d_skip[i], od_ssm_norm_g[i], od_ret_gn_g[i], od_ret_gn_b[i], od_w_out[i],
                            ln_mix_g[layer], ln_mix_b[layer], od_tables, od_consts, alpha, batch, seq)
        x2 = _ffn(x2, ffn_w_gate[layer].astype(BF16), ffn_w_up[layer].astype(BF16),
                  ffn_w_down[layer].astype(BF16), _row(ln_ffn_g[layer]), _row(ln_ffn_b[layer]), alpha)
    return x2.reshape(batch, seq, d_model)
```

```python
import functools
import math

import jax
import jax.numpy as jnp
from jax import lax
from jax.experimental import pallas as pl
from jax.experimental.pallas import tpu as pltpu

F32 = jnp.float32
BF16 = jnp.bfloat16

A_HEADS = 8
A_KV_HEADS = 2
A_HEAD_DIM = 64
A_WINDOW = 128
ROPE_THETA = 500000.0
A_ROPE_DIM = A_HEAD_DIM // 4
B_CONV = 31
C_HEADS = 16
C_HEAD_DIM = 64
C_GROUPS = 2
C_STATE = 128
C_CONV = 4
C_CHUNK = 128
D_HEADS = 4
D_QK_DIM = 128
D_V_DIM = 256
D_CHUNK = 128
RET_THETA = 10000.0
NORM_EPS = 1e-5

LANES = 128
SUBLANES = 8
VMEM_LIMIT = 56 * 1024 * 1024

ROW_TILE = 512
ATTN_TILE = 512
CONV_TILE = 256
CONV_ROWS = 32
CONV_HALO = 32
FFN_CHUNK = 256
PROJ_COLS = 256
CONV4_ROWS = 256
SCAN_TILE = 256


def _cparams(*sem):
    return pltpu.CompilerParams(dimension_semantics=sem, vmem_limit_bytes=VMEM_LIMIT)


def _resident(shape):
    nd = len(shape)
    return pl.BlockSpec(shape, lambda *_: (0,) * nd, pipeline_mode=pl.Buffered(1))


def _layer_norm(y, g, b):
    mu = jnp.mean(y, axis=-1, keepdims=True)
    d = y - mu
    var = jnp.mean(d * d, axis=-1, keepdims=True)
    return d * lax.rsqrt(var + NORM_EPS) * g + b


def _silu(v):
    return v * jax.nn.sigmoid(v)


def _dot(a, b):
    return jnp.dot(a, b, preferred_element_type=F32)


def _dot_nt(a, b):
    return lax.dot_general(a, b, (((1,), (1,)), ((), ())), preferred_element_type=F32)


def _dot_tn(a, b):
    return lax.dot_general(a, b, (((0,), (0,)), ((), ())), preferred_element_type=F32)


def _split_bf16(v, parts):
    out = []
    r = v
    for _ in range(parts):
        p = r.astype(BF16)
        out.append(p)
        r = r - p.astype(F32)
    return out


def _even_in_kernel(x_ref, wq_ref, wk_ref, wv_ref, wu_ref, cos_ref, sa_ref, sb_ref,
                    q_ref, k_ref, v_ref, h_ref):
    xb = x_ref[...].astype(BF16)
    c, sa, sb = cos_ref[...], sa_ref[...], sb_ref[...]
    rot_dim = A_ROPE_DIM // 2

    def rot(t):
        return t * c + pltpu.roll(t, rot_dim, 1) * sa + pltpu.roll(t, LANES - rot_dim, 1) * sb

    qscale = A_HEAD_DIM ** -0.5
    q = _dot(xb, wq_ref[...])
    for j in range(q.shape[1] // LANES):
        q_ref[:, j * LANES:(j + 1) * LANES] = (rot(q[:, j * LANES:(j + 1) * LANES]) * qscale).astype(BF16)
    k = _dot(xb, wk_ref[...])
    for j in range(k.shape[1] // LANES):
        k_ref[:, j * LANES:(j + 1) * LANES] = rot(k[:, j * LANES:(j + 1) * LANES]).astype(BF16)
    v_ref[...] = _dot(xb, wv_ref[...]).astype(BF16)
    u = _dot(xb, wu_ref[...])
    nb = u.shape[1] // 2
    h_ref[...] = u[:, :nb] * jax.nn.sigmoid(u[:, nb:])


def _even_in(x2, wq, wk, wv, wu, cos_t, sa_t, sb_t, seq):
    T, D = x2.shape
    tm = min(ROW_TILE, seq)
    nseq = seq // tm
    row = lambda i: (i, 0)
    tab = pl.BlockSpec((tm, LANES), lambda i: (i % nseq, 0))
    return pl.pallas_call(
        _even_in_kernel,
        grid=(T // tm,),
        in_specs=[pl.BlockSpec((tm, D), row), _resident(wq.shape), _resident(wk.shape),
                  _resident(wv.shape), _resident(wu.shape), tab, tab, tab],
        out_specs=[pl.BlockSpec((tm, wq.shape[1]), row), pl.BlockSpec((tm, wk.shape[1]), row),
                   pl.BlockSpec((tm, wv.shape[1]), row), pl.BlockSpec((tm, wu.shape[1] // 2), row)],
        out_shape=[jax.ShapeDtypeStruct((T, wq.shape[1]), BF16), jax.ShapeDtypeStruct((T, wk.shape[1]), BF16),
                   jax.ShapeDtypeStruct((T, wv.shape[1]), BF16), jax.ShapeDtypeStruct((T, wu.shape[1] // 2), F32)],
        compiler_params=_cparams("parallel"),
        name="even_in",
    )(x2, wq, wk, wv, wu, cos_t, sa_t, sb_t)


def _attn_kernel(sink_ref, q_ref, kp_ref, kc_ref, vp_ref, vc_ref, o_ref):
    n = pl.program_id(1)
    W = A_WINDOW
    G = A_HEADS // A_KV_HEADS
    nsub = q_ref.shape[0] // W
    r = lax.broadcasted_iota(jnp.int32, (G * W, 2 * W), 0)
    i = jnp.bitwise_and(r, W - 1)
    c = lax.broadcasted_iota(jnp.int32, (G * W, 2 * W), 1)
    band = jnp.logical_and(c > i, c <= i + W)
    neg = jnp.float32(-jnp.inf)
    bias_inner = jnp.where(band, 0.0, neg)
    bias_first = jnp.where(jnp.logical_and(band, jnp.logical_or(c >= W, n > 0)), 0.0, neg)
    rg = lax.broadcasted_iota(jnp.int32, (G * W, 1), 0) // W
    lane = lax.broadcasted_iota(jnp.int32, (W, LANES), 1)
    lo = lane < A_HEAD_DIM
    ones = jnp.ones((2 * W, LANES), BF16)
    for sub in range(nsub):
        rows = slice(sub * W, (sub + 1) * W)
        prow = slice((sub - 1) * W, sub * W)
        bias = bias_first if sub == 0 else bias_inner
        for h in range(A_KV_HEADS):
            kl = slice(h * LANES, (h + 1) * LANES)
            kprev = kp_ref[:, kl] if sub == 0 else kc_ref[prow, kl]
            kcat = jnp.concatenate([kprev, kc_ref[rows, kl]], axis=0)
            parts = []
            for jj in range(G // 2):
                j = h * (G // 2) + jj
                q2 = q_ref[rows, j * LANES:(j + 1) * LANES]
                parts.append(jnp.where(lo, q2, jnp.zeros_like(q2)))
                parts.append(jnp.where(lo, jnp.zeros_like(q2), q2))
            s = _dot_nt(jnp.concatenate(parts, axis=0), kcat) + bias
            sink = jnp.full((G * W, 1), sink_ref[h * G], F32)
            for gi in range(1, G):
                sink = jnp.where(rg == gi, sink_ref[h * G + gi], sink)
            m = jnp.maximum(jnp.max(jnp.maximum(s[:, :W], s[:, W:]), axis=-1, keepdims=True), sink)
            p = jnp.exp(s - m).astype(BF16)
            esink = jnp.exp(sink - m)
            vaug = []
            for half in range(2):
                vl = slice((2 * h + half) * LANES, (2 * h + half + 1) * LANES)
                vprev = vp_ref[:, vl] if sub == 0 else vc_ref[prow, vl]
                vcat = jnp.concatenate([vprev, vc_ref[rows, vl]], axis=0)
                vaug.append(jnp.concatenate([vcat, ones], axis=1))
            for jj in range(G // 2):
                j = h * (G // 2) + jj
                he = slice((2 * jj) * W, (2 * jj + 1) * W)
                ho = slice((2 * jj + 1) * W, (2 * jj + 2) * W)
                re = _dot(p[he], vaug[0])
                ro = _dot(p[ho], vaug[1])
                acc = re[:, :LANES] + ro[:, :LANES]
                d = jnp.where(lo, re[:, LANES:] + esink[he], ro[:, LANES:] + esink[ho])
                o_ref[rows, j * LANES:(j + 1) * LANES] = (acc * (1.0 / d)).astype(BF16)


def _attention(sinks, q, k, v, batch, seq):
    W = A_WINDOW
    tq = min(ATTN_TILE, seq)
    per = tq // W
    nt = seq // tq
    cur = lambda b, n: (b * nt + n, 0)
    prev = lambda b, n: (jnp.maximum((b * nt + n) * per - 1, 0), 0)
    return pl.pallas_call(
        _attn_kernel,
        grid=(batch, nt),
        in_specs=[pl.BlockSpec(memory_space=pltpu.SMEM),
                  pl.BlockSpec((tq, q.shape[1]), cur),
                  pl.BlockSpec((W, k.shape[1]), prev), pl.BlockSpec((tq, k.shape[1]), cur),
                  pl.BlockSpec((W, v.shape[1]), prev), pl.BlockSpec((tq, v.shape[1]), cur)],
        out_specs=pl.BlockSpec((tq, q.shape[1]), cur),
        out_shape=jax.ShapeDtypeStruct(q.shape, BF16),
        compiler_params=_cparams("parallel", "parallel"),
        name="swa_attention",
    )(sinks, q, k, k, v, v)


def _conv_kernel(hp_ref, hc_ref, w_ref, b_ref, g_ref, beta_ref, o_ref, var_ref):
    n = pl.program_id(1)
    tc, C = hc_ref.shape
    ext_rows = CONV_HALO + tc
    for c0 in range(0, C, LANES):
        cols = slice(c0, c0 + LANES)
        ext = jnp.concatenate([jnp.where(n > 0, hp_ref[:, cols], 0.0), hc_ref[:, cols]], axis=0)
        var_ref[0, :, :, cols] = ext.reshape(ext_rows // SUBLANES, SUBLANES, LANES)
        for r in range(1, SUBLANES):
            sh = pltpu.roll(ext, ext_rows - r, 0)
            var_ref[r, :, :, cols] = sh.reshape(ext_rows // SUBLANES, SUBLANES, LANES)
    first = CONV_HALO - (B_CONV - 1)
    nt = CONV_ROWS // SUBLANES
    for r0 in range(0, tc, CONV_ROWS):
        acc = None
        for kk in range(B_CONV):
            d = first + kk
            s = r0 // SUBLANES + d // SUBLANES
            t = w_ref[kk] * var_ref[d % SUBLANES, s:s + nt]
            acc = t + b_ref[...] if acc is None else acc + t
        y = _layer_norm(acc, g_ref[...], beta_ref[...])
        o_ref[r0:r0 + CONV_ROWS, :] = _silu(y).reshape(CONV_ROWS, C).astype(BF16)


def _conv_module(h, w, b, g, beta, batch, seq):
    T, C = h.shape
    tc = min(CONV_TILE, seq)
    nt = seq // tc
    per = tc // CONV_HALO
    cur = lambda bi, n: (bi * nt + n, 0)
    prev = lambda bi, n: (jnp.maximum((bi * nt + n) * per - 1, 0), 0)
    return pl.pallas_call(
        _conv_kernel,
        grid=(batch, nt),
        in_specs=[pl.BlockSpec((CONV_HALO, C), prev), pl.BlockSpec((tc, C), cur),
                  _resident(w.shape), _resident(b.shape), _resident(g.shape), _resident(beta.shape)],
        out_specs=pl.BlockSpec((tc, C), cur),
        out_shape=jax.ShapeDtypeStruct((T, C), BF16),
        scratch_shapes=[pltpu.VMEM((SUBLANES, (CONV_HALO + tc) // SUBLANES, SUBLANES, C), F32)],
        compiler_params=_cparams("parallel", "parallel"),
        name="conv_module",
    )(h, h, w, b, g, beta)


def _out_ffn_kernel(alpha, ya_ref, yb_ref, x_ref, wa_ref, wb_ref, g1_ref, b1_ref,
                    wg_ref, wu_ref, wd_ref, g2_ref, b2_ref, o_ref, x1_ref, xb_ref):
    mix = _dot(ya_ref[...], wa_ref[...]) + _dot(yb_ref[...], wb_ref[...])
    x1 = _layer_norm(alpha * x_ref[...] + mix, g1_ref[...], b1_ref[...])
    x1_ref[...] = x1
    xb_ref[...] = x1.astype(BF16)
    hidden = wg_ref.shape[1]
    acc = None
    for c0 in range(0, hidden, FFN_CHUNK):
        c1 = min(c0 + FFN_CHUNK, hidden)
        gate = _dot(xb_ref[...], wg_ref[:, c0:c1])
        up = _dot(xb_ref[...], wu_ref[:, c0:c1])
        t = _dot((_silu(gate) * up).astype(BF16), wd_ref[c0:c1, :])
        acc = t if acc is None else acc + t
    o_ref[...] = _layer_norm(alpha * x1_ref[...] + acc, g2_ref[...], b2_ref[...])


def _out_ffn(ya, yb, x2, wa, wb, g1, b1, wg, wu, wd, g2, b2, alpha):
    T, D = x2.shape
    tm = min(ROW_TILE, T)
    row = lambda i: (i, 0)
    consts = (wa, wb, g1, b1, wg, wu, wd, g2, b2)
    return pl.pallas_call(
        functools.partial(_out_ffn_kernel, alpha),
        grid=(T // tm,),
        in_specs=[pl.BlockSpec((tm, ya.shape[1]), row), pl.BlockSpec((tm, yb.shape[1]), row),
                  pl.BlockSpec((tm, D), row)] + [_resident(t.shape) for t in consts],
        out_specs=pl.BlockSpec((tm, D), row),
        out_shape=jax.ShapeDtypeStruct((T, D), F32),
        scratch_shapes=[pltpu.VMEM((tm, D), F32), pltpu.VMEM((tm, D), BF16)],
        compiler_params=_cparams("parallel"),
        name="out_ffn",
    )(ya, yb, x2, *consts)


def _odd_in_kernel(nseq, x_ref, wz_ref, wx_ref, wdt_ref, wq_ref, wk_ref, wv_ref, wg_ref, cos_ref, sin_ref,
                   cw_ref, cb_ref, dtb_ref,
                   z_ref, xs_ref, bc_ref, dt_ref, q_ref, k_ref, v_ref, g_ref, buf_ref):
    i = pl.program_id(0)
    tm = x_ref.shape[0]
    inner = xs_ref.shape[1]
    xb = x_ref[...].astype(BF16)

    def col_blocks(w_ref):
        for c0 in range(0, w_ref.shape[1], PROJ_COLS):
            c1 = min(c0 + PROJ_COLS, w_ref.shape[1])
            yield c0, c1, _dot(xb, w_ref[:, c0:c1])

    @pl.when(i % nseq == 0)
    def _():
        buf_ref[0:SUBLANES, :] = jnp.zeros((SUBLANES, buf_ref.shape[1]), F32)

    @pl.when(i % nseq != 0)
    def _():
        buf_ref[0:SUBLANES, :] = buf_ref[tm:tm + SUBLANES, :]

    rblk = min(CONV4_ROWS, tm)
    ntile = (SUBLANES + rblk) // SUBLANES

    def conv_block(r0, c0):
        cols = slice(c0, c0 + LANES)
        ext = buf_ref[r0:r0 + SUBLANES + rblk, cols]
        acc = cb_ref[:, cols] + cw_ref[C_CONV - 1, :, cols] * ext.reshape(ntile, SUBLANES, LANES)
        for kk in range(1, C_CONV):
            sh = pltpu.roll(ext, kk, 0).reshape(ntile, SUBLANES, LANES)
            acc = acc + cw_ref[C_CONV - 1 - kk, :, cols] * sh
        y = _silu(acc[1:]).reshape(rblk, LANES)
        if c0 < inner:
            xs_ref[r0:r0 + rblk, cols] = y
        else:
            bc_ref[r0:r0 + rblk, c0 - inner:c0 - inner + LANES] = y.astype(BF16)

    for c0, c1, r in col_blocks(wx_ref):
        buf_ref[SUBLANES:SUBLANES + tm, c0:c1] = r
        for cc in range(c0, c1, LANES):
            for r0 in range(0, tm, rblk):
                conv_block(r0, cc)

    for c0, c1, r in col_blocks(wz_ref):
        z_ref[:, c0:c1] = _silu(r)

    c, s = cos_ref[...], sin_ref[...]
    kscale = D_QK_DIM ** -0.5

    def rot(t):
        return t * c + pltpu.roll(t, D_QK_DIM // 2, 1) * s

    for c0, c1, r in col_blocks(wq_ref):
        for cc in range(c0, c1, D_QK_DIM):
            q_ref[:, cc:cc + D_QK_DIM] = rot(r[:, cc - c0:cc - c0 + D_QK_DIM]).astype(BF16)
    for c0, c1, r in col_blocks(wk_ref):
        for cc in range(c0, c1, D_QK_DIM):
            k_ref[:, cc:cc + D_QK_DIM] = rot(r[:, cc - c0:cc - c0 + D_QK_DIM]) * kscale
    for c0, c1, r in col_blocks(wv_ref):
        v_ref[:, c0:c1] = r.astype(BF16)
    for c0, c1, r in col_blocks(wg_ref):
        g_ref[:, c0:c1] = _silu(r)
    dtv = _dot(xb, wdt_ref[...]) + dtb_ref[...]
    dt_ref[...] = jnp.maximum(dtv, 0.0) + jnp.log1p(jnp.exp(-jnp.abs(dtv)))


def _odd_in(x2, wz, wx, wdt, wq, wk, wv, wg, cos_t, sin_t, cw, cb, dtb, seq):
    T, D = x2.shape
    tm = min(ROW_TILE, seq)
    nseq = seq // tm
    inner = wz.shape[1]
    row = lambda i: (i, 0)
    tab = pl.BlockSpec((tm, LANES), lambda i: (i % nseq, 0))
    ws = (wz, wx, wdt, wq, wk, wv, wg)
    consts = (cw, cb, dtb)
    widths = (inner, inner, wx.shape[1] - inner, wdt.shape[1], wq.shape[1], wk.shape[1], wv.shape[1], wg.shape[1])
    dts = (F32, F32, BF16, F32, BF16, F32, BF16, F32)
    return pl.pallas_call(
        functools.partial(_odd_in_kernel, nseq),
        grid=(T // tm,),
        in_specs=[pl.BlockSpec((tm, D), row)] + [_resident(w.shape) for w in ws] + [tab, tab]
                 + [_resident(t.shape) for t in consts],
        out_specs=[pl.BlockSpec((tm, n), row) for n in widths],
        out_shape=[jax.ShapeDtypeStruct((T, n), dt) for n, dt in zip(widths, dts)],
        scratch_shapes=[pltpu.VMEM((SUBLANES + tm, wx.shape[1]), F32)],
        compiler_params=_cparams("arbitrary"),
        name="odd_in",
    )(x2, *ws, cos_t, sin_t, *consts)


def _ssd_kernel(z_ref, xs_ref, bc_ref, dt_ref, alog_ref, dskip_ref, ng_ref,
                expand_ref, tri_ref, o_ref, state_ref, y_ref):
    @pl.when(pl.program_id(1) == 0)
    def _():
        state_ref[...] = jnp.zeros_like(state_ref)

    L = C_CHUNK
    for r0 in range(0, z_ref.shape[0], L):
        rows = slice(r0, r0 + L)
        _ssd_chunk(z_ref.at[rows], xs_ref.at[rows], bc_ref.at[rows], dt_ref.at[rows], alog_ref, dskip_ref,
                   ng_ref, expand_ref, tri_ref, o_ref.at[rows], state_ref, y_ref)


def _ssd_chunk(z_ref, xs_ref, bc_ref, dt_ref, alog_ref, dskip_ref, ng_ref, expand_ref, tri_ref,
               o_ref, state_ref, y_ref):
    L = C_CHUNK
    gw = C_STATE
    hpg = C_HEADS // C_GROUPS
    gcols = hpg * C_HEAD_DIM
    xs = xs_ref[...]
    xs_b = xs.astype(BF16)

    dt = dt_ref[...]
    a = -jnp.exp(alog_ref[...])
    acs = _cumsum_rows(dt * a, tri_ref[...])
    last = acs[L - 1:L, :]
    acs_t = acs.T
    dt_t = dt.T
    to_state_e = _expand_heads(dt * jnp.exp(last - acs), expand_ref[...])
    from_start_e = _expand_heads(jnp.exp(acs), expand_ref[...])
    cdec_e = from_start_e[L - 1:L, :]
    wst = (xs * to_state_e).astype(BF16)

    row = lax.broadcasted_iota(jnp.int32, (L, L), 0)
    col = lax.broadcasted_iota(jnp.int32, (L, L), 1)
    causal = col <= row
    lane = lax.broadcasted_iota(jnp.int32, (L, LANES), 1)
    lo = lane < C_HEAD_DIM
    zero_b = jnp.zeros((L, LANES), BF16)
    quad = 2 * LANES

    for g in range(C_GROUPS):
        bm = bc_ref[:, g * gw:(g + 1) * gw]
        cm = bc_ref[:, (C_GROUPS + g) * gw:(C_GROUPS + g + 1) * gw]
        cbm = _dot_nt(cm, bm)
        gs = slice(g * gcols, (g + 1) * gcols)
        prev = state_ref[:, gs]
        y_off = _dot(cm, prev.astype(BF16)) * from_start_e[:, gs]
        new = _dot_tn(bm, wst[:, gs])
        state_ref[:, gs] = prev * cdec_e[:, gs] + new
        for qd in range(gcols // quad):
            ms, xr = [], []
            for jp in range(2):
                c0 = g * gcols + qd * quad + jp * LANES
                xp = xs_b[:, c0:c0 + LANES]
                for half in range(2):
                    h = (c0 // C_HEAD_DIM) + half
                    seg = acs[:, h:h + 1] - acs_t[h:h + 1, :]
                    dec = jnp.exp(jnp.where(causal, seg, -jnp.inf))
                    ms.append((cbm * dec * dt_t[h:h + 1, :]).astype(BF16))
                    xh = jnp.where(lo if half == 0 else jnp.logical_not(lo), xp, zero_b)
                    xr.append(jnp.concatenate([xh, zero_b] if jp == 0 else [zero_b, xh], axis=1))
            cols = slice(g * gcols + qd * quad, g * gcols + (qd + 1) * quad)
            y = (_dot(jnp.concatenate(ms, axis=1), jnp.concatenate(xr, axis=0))
                 + y_off[:, qd * quad:(qd + 1) * quad] + xs[:, cols] * dskip_ref[:, cols])
            y_ref[:, cols] = y * z_ref[:, cols]

    for g in range(C_GROUPS):
        gs = slice(g * gcols, (g + 1) * gcols)
        yg = y_ref[:, gs]
        ms = jnp.mean(yg * yg, axis=-1, keepdims=True)
        o_ref[:, gs] = (yg * lax.rsqrt(ms + NORM_EPS) * ng_ref[:, gs]).astype(BF16)


def _expand_heads(v, expand3):
    lane = lax.broadcasted_iota(jnp.int32, v.shape, 1)
    hi, mid, low = _split_bf16(jnp.where(lane < C_HEADS, v, 0.0), 3)
    packed = (hi.astype(F32) + pltpu.roll(mid.astype(F32), C_HEADS, 1)
              + pltpu.roll(low.astype(F32), 2 * C_HEADS, 1))
    return _dot(packed.astype(BF16), expand3)


def _cumsum_rows(v, tri):
    acc = None
    for p in _split_bf16(v, 3):
        t = _dot(tri, p)
        acc = t if acc is None else acc + t
    return acc


def _ssd(z, xs, bc, dt, alog, dskip, ng, expand, tri, batch, seq):
    tr = min(SCAN_TILE, seq)
    nt = seq // tr
    cur = lambda b, c: (b * nt + c, 0)
    consts = (alog, dskip, ng, expand, tri)
    return pl.pallas_call(
        _ssd_kernel,
        grid=(batch, nt),
        in_specs=[pl.BlockSpec((tr, t.shape[1]), cur) for t in (z, xs, bc, dt)]
                 + [_resident(t.shape) for t in consts],
        out_specs=pl.BlockSpec((tr, z.shape[1]), cur),
        out_shape=jax.ShapeDtypeStruct(z.shape, BF16),
        scratch_shapes=[pltpu.VMEM((C_STATE, z.shape[1]), F32),
                        pltpu.VMEM((C_CHUNK, z.shape[1]), F32)],
        compiler_params=_cparams("arbitrary", "arbitrary"),
        name="ssd_scan",
    )(z, xs, bc, dt, *consts)


def _ret_kernel(q_ref, k_ref, v_ref, g_ref, dmat_ref, toend_ref, fstart_ref, cdec_ref, gng_ref, gnb_ref,
                o_ref, state_ref):
    @pl.when(pl.program_id(1) == 0)
    def _():
        state_ref[...] = jnp.zeros_like(state_ref)

    L = D_CHUNK
    for r0 in range(0, q_ref.shape[0], L):
        rows = slice(r0, r0 + L)
        for h in range(D_HEADS):
            ks = slice(h * D_QK_DIM, (h + 1) * D_QK_DIM)
            vs = slice(h * D_V_DIM, (h + 1) * D_V_DIM)
            qh = q_ref[rows, ks]
            kh = k_ref[rows, ks]
            vh = v_ref[rows, vs]
            scores = _dot_nt(qh, kh.astype(BF16)) * dmat_ref[h]
            prev = state_ref[:, vs]
            y = _dot(scores.astype(BF16), vh) + _dot(qh, prev.astype(BF16)) * fstart_ref[:, vs]
            kd = (kh * toend_ref[:, ks]).astype(BF16)
            state_ref[:, vs] = prev * cdec_ref[:, vs] + _dot_tn(kd, vh)
            mu = jnp.mean(y, axis=-1, keepdims=True)
            d = y - mu
            var = jnp.mean(d * d, axis=-1, keepdims=True)
            yn = d * lax.rsqrt(var + NORM_EPS) * gng_ref[:, vs] + gnb_ref[:, vs]
            o_ref[rows, vs] = (g_ref[rows, vs] * yn).astype(BF16)


def _retention(q, k, v, g, dmat, toend, fstart, cdec, gng, gnb, batch, seq):
    L = min(SCAN_TILE, seq)
    nc = seq // L
    cur = lambda b, c: (b * nc + c, 0)
    consts = (dmat, toend, fstart, cdec, gng, gnb)
    return pl.pallas_call(
        _ret_kernel,
        grid=(batch, nc),
        in_specs=[pl.BlockSpec((L, q.shape[1]), cur), pl.BlockSpec((L, k.shape[1]), cur),
                  pl.BlockSpec((L, v.shape[1]), cur), pl.BlockSpec((L, g.shape[1]), cur)]
                 + [_resident(t.shape) for t in consts],
        out_specs=pl.BlockSpec((L, v.shape[1]), cur),
        out_shape=jax.ShapeDtypeStruct(v.shape, BF16),
        scratch_shapes=[pltpu.VMEM((D_QK_DIM, v.shape[1]), F32)],
        compiler_params=_cparams("arbitrary", "arbitrary"),
        name="retention_scan",
    )(q, k, v, g, *consts)


def _even_tables(seq):
    half = A_ROPE_DIM // 2
    inv = jnp.power(jnp.float32(ROPE_THETA), -jnp.arange(half, dtype=F32) / half)
    ang = jnp.arange(seq, dtype=F32)[:, None] * inv[None, :]
    cos, sin = jnp.cos(ang), jnp.sin(ang)
    ones = jnp.ones((seq, A_HEAD_DIM - 2 * half), F32)
    zeros = jnp.zeros((seq, A_HEAD_DIM - 2 * half), F32)
    zh = jnp.zeros((seq, half), F32)
    c = jnp.concatenate([cos, cos, ones], axis=1)
    sa = jnp.concatenate([zh, sin, zeros], axis=1)
    sb = jnp.concatenate([-sin, zh, zeros], axis=1)
    rep = LANES // A_HEAD_DIM
    return jnp.tile(c, (1, rep)), jnp.tile(sa, (1, rep)), jnp.tile(sb, (1, rep))


def _ret_tables(seq):
    half = D_QK_DIM // 2
    inv = 1.0 / jnp.power(jnp.float32(RET_THETA), jnp.linspace(0.0, 1.0, half, dtype=F32))
    ang = jnp.arange(seq, dtype=F32)[:, None] * inv[None, :]
    cos, sin = jnp.cos(ang), jnp.sin(ang)
    L = D_CHUNK
    log_gamma = jnp.log(1.0 - jnp.power(2.0, -5.0 - jnp.arange(D_HEADS, dtype=F32)))
    idx = jnp.arange(L, dtype=F32)
    rel = idx[:, None] - idx[None, :]
    dmat = jnp.where(rel[None] >= 0, jnp.exp(log_gamma[:, None, None] * jnp.maximum(rel, 0.0)[None]), 0.0)
    to_end = jnp.exp(log_gamma[None, :] * (L - 1 - idx)[:, None])
    from_start = jnp.exp(log_gamma[None, :] * (idx + 1.0)[:, None])
    chunk_decay = jnp.exp(log_gamma * L)
    return (jnp.concatenate([cos, cos], axis=1), jnp.concatenate([-sin, sin], axis=1), dmat,
            jnp.repeat(to_end, D_QK_DIM, axis=1), jnp.repeat(from_start, D_V_DIM, axis=1),
            jnp.repeat(chunk_decay, D_V_DIM)[None, :])


def _row(v):
    return v.astype(F32)[None, :]


def _sublane_rep(v):
    v = v.astype(F32)
    return jnp.broadcast_to(v[..., None, :], v.shape[:-1] + (SUBLANES, v.shape[-1]))


def _pad_lanes(v, fill=0.0):
    pad = (-v.shape[-1]) % LANES
    return jnp.pad(v, [(0, 0)] * (v.ndim - 1) + [(0, pad)], constant_values=fill)


def _even_mixer(x2, w_in, sinks, dw_w, dw_b, cn_g, cn_b, w_out, tables, batch, seq):
    nq = A_HEADS * A_HEAD_DIM
    nkv = A_KV_HEADS * A_HEAD_DIM
    wq = w_in[:, :nq].astype(BF16)
    wk = w_in[:, nq:nq + nkv].astype(BF16)
    wv = w_in[:, nq + nkv:nq + 2 * nkv].astype(BF16)
    wu = w_in[:, nq + 2 * nkv:].astype(BF16)
    D = w_in.shape[0]
    wk = jnp.repeat(wk.reshape(D, A_KV_HEADS, 1, A_HEAD_DIM), 2, axis=2).reshape(D, 2 * nkv)
    wv4 = wv.reshape(D, A_KV_HEADS, 1, A_HEAD_DIM)
    zv = jnp.zeros_like(wv4)
    wv = jnp.concatenate([wv4, zv, zv, wv4], axis=2).reshape(D, 4 * nkv)
    q, k, v, hglu = _even_in(x2, wq, wk, wv, wu, *tables, seq)
    y_attn = _attention(sinks.astype(F32), q, k, v, batch, seq)
    y_conv = _conv_module(hglu, _sublane_rep(dw_w), _sublane_rep(dw_b), _sublane_rep(cn_g), _sublane_rep(cn_b),
                          batch, seq)
    wo = w_out.astype(BF16)
    return y_attn, y_conv, wo[:nq], wo[nq:]


def _odd_mixer(x2, w_in, conv_w, conv_b, dt_bias, a_log, d_skip, ssm_norm_g, gn_g, gn_b, w_out,
               tables, consts, batch, seq):
    inner = C_HEADS * C_HEAD_DIM
    xbc_w = inner + 2 * C_GROUPS * C_STATE
    nqk = D_HEADS * D_QK_DIM
    nv = D_HEADS * D_V_DIM
    sizes = (inner, xbc_w, C_HEADS, nqk, nqk, nv, nv)
    offs = [0]
    for s in sizes:
        offs.append(offs[-1] + s)
    wb = w_in.astype(BF16)
    wz, wx, wdt, wq, wk, wv, wg = [wb[:, offs[i]:offs[i + 1]] for i in range(len(sizes))]
    wdt = _pad_lanes(wdt)
    cos_t, sin_t, dmat, toend, fstart, cdec = tables
    expand, tri = consts
    z, xs, bc, dt, rq, rk, rv, rg = _odd_in(x2, wz, wx, wdt, wq, wk, wv, wg, cos_t, sin_t,
                                            _sublane_rep(conv_w), _sublane_rep(conv_b),
                                            _pad_lanes(_row(dt_bias)), seq)
    y_ssm = _ssd(z, xs, bc, dt, _pad_lanes(_row(a_log)), _row(jnp.repeat(d_skip.astype(F32), C_HEAD_DIM)),
                 _row(ssm_norm_g), expand, tri, batch, seq)
    y_ret = _retention(rq, rk, rv, rg, dmat, toend, fstart, cdec, _row(gn_g), _row(gn_b), batch, seq)
    wo = w_out.astype(BF16)
    return y_ssm, y_ret, wo[:inner], wo[inner:]


def _ssd_consts():
    k = jnp.arange(LANES)[:, None]
    cols = jnp.arange(C_HEADS * C_HEAD_DIM)[None, :] // C_HEAD_DIM
    expand = jnp.logical_and(k % C_HEADS == cols, k < 3 * C_HEADS).astype(BF16)
    i = jnp.arange(C_CHUNK)
    tri = (i[None, :] <= i[:, None]).astype(BF16)
    return expand, tri


def kernel(x, ln_mix_g, ln_mix_b, ln_ffn_g, ln_ffn_b, ffn_w_gate, ffn_w_up, ffn_w_down, ev_w_in, ev_sinks, ev_dw_w, ev_dw_b, ev_cn_g, ev_cn_b, ev_w_out, od_w_in, od_conv_w, od_conv_b, od_dt_bias, od_a_log, od_d_skip, od_ssm_norm_g, od_ret_gn_g, od_ret_gn_b, od_w_out):
    batch, seq, d_model = x.shape
    depth = ln_mix_g.shape[0]
    alpha = float((2 * depth) ** 0.25)
    ev_tables = _even_tables(seq)
    od_tables = _ret_tables(seq)
    od_consts = _ssd_consts()
    x2 = x.reshape(batch * seq, d_model)
    for layer in range(depth):
        i = layer // 2
        if layer % 2 == 0:
            mixed = _even_mixer(x2, ev_w_in[i], ev_sinks[i], ev_dw_w[i], ev_dw_b[i], ev_cn_g[i], ev_cn_b[i],
                                ev_w_out[i], ev_tables, batch, seq)
        else:
            mixed = _odd_mixer(x2, od_w_in[i], od_conv_w[i], od_conv_b[i], od_dt_bias[i], od_a_log[i],
                               od_d_skip[i], od_ssm_norm_g[i], od_ret_gn_g[i], od_ret_gn_b[i], od_w_out[i],
                               od_tables, od_consts, batch, seq)
        x2 = _mixer_out_ffn(mixed, x2, ln_mix_g[layer], ln_mix_b[layer], ffn_w_gate[layer], ffn_w_up[layer],
                            ffn_w_down[layer], ln_ffn_g[layer], ln_ffn_b[layer], alpha)
    return x2.reshape(batch, seq, d_model)


def _mixer_out_ffn(mixed, x2, ln1_g, ln1_b, w_gate, w_up, w_down, ln2_g, ln2_b, alpha):
    ya, yb, wa, wb = mixed
    return _out_ffn(ya, yb, x2, wa, wb, _row(ln1_g), _row(ln1_b), w_gate.astype(BF16), w_up.astype(BF16),
                    w_down.astype(BF16), _row(ln2_g), _row(ln2_b), alpha)
```

```python
import functools
import math

import jax
import jax.numpy as jnp
from jax import lax
from jax.experimental import pallas as pl
from jax.experimental.pallas import tpu as pltpu

F32 = jnp.float32
BF16 = jnp.bfloat16

A_HEADS = 8
A_KV_HEADS = 2
A_HEAD_DIM = 64
A_WINDOW = 128
ROPE_THETA = 500000.0
A_ROPE_DIM = A_HEAD_DIM // 4
B_CONV = 31
C_HEADS = 16
C_HEAD_DIM = 64
C_GROUPS = 2
C_STATE = 128
C_CONV = 4
C_CHUNK = 128
D_HEADS = 4
D_QK_DIM = 128
D_V_DIM = 256
D_CHUNK = 128
RET_THETA = 10000.0
NORM_EPS = 1e-5

LANES = 128
SUBLANES = 8
VMEM_LIMIT = 56 * 1024 * 1024

ROW_TILE = 512
ATTN_TILE = 512
CONV_TILE = 512
CONV_ROWS = 32
CONV_HALO = 32
FFN_CHUNK = 256
PROJ_COLS = 256
CONV4_ROWS = 256
SCAN_TILE = 512


def _cparams(*sem):
    return pltpu.CompilerParams(dimension_semantics=sem, vmem_limit_bytes=VMEM_LIMIT)


def _resident(shape):
    nd = len(shape)
    return pl.BlockSpec(shape, lambda *_: (0,) * nd, pipeline_mode=pl.Buffered(1))


def _layer_norm(y, g, b):
    mu = jnp.mean(y, axis=-1, keepdims=True)
    d = y - mu
    var = jnp.mean(d * d, axis=-1, keepdims=True)
    return d * lax.rsqrt(var + NORM_EPS) * g + b


def _silu(v):
    h = 0.5 * v
    return h + h * jnp.tanh(h)


def _dot(a, b):
    return jnp.dot(a, b, preferred_element_type=F32)


def _dot_nt(a, b):
    return lax.dot_general(a, b, (((1,), (1,)), ((), ())), preferred_element_type=F32)


def _dot_tn(a, b):
    return lax.dot_general(a, b, (((0,), (0,)), ((), ())), preferred_element_type=F32)


def _split_bf16(v, parts):
    out = []
    r = v
    for _ in range(parts):
        p = r.astype(BF16)
        out.append(p)
        r = r - p.astype(F32)
    return out


def _even_in_kernel(x_ref, wq_ref, wk_ref, wv_ref, wu_ref, cos_ref, sa_ref, sb_ref,
                    q_ref, k_ref, v_ref, h_ref):
    xb = x_ref[...].astype(BF16)
    c, sa, sb = cos_ref[...], sa_ref[...], sb_ref[...]
    rot_dim = A_ROPE_DIM // 2

    def rot(t):
        return t * c + pltpu.roll(t, rot_dim, 1) * sa + pltpu.roll(t, LANES - rot_dim, 1) * sb

    qscale = A_HEAD_DIM ** -0.5
    q = _dot(xb, wq_ref[...])
    for j in range(q.shape[1] // LANES):
        q_ref[:, j * LANES:(j + 1) * LANES] = (rot(q[:, j * LANES:(j + 1) * LANES]) * qscale).astype(BF16)
    k = _dot(xb, wk_ref[...])
    for j in range(k.shape[1] // LANES):
        k_ref[:, j * LANES:(j + 1) * LANES] = rot(k[:, j * LANES:(j + 1) * LANES]).astype(BF16)
    v_ref[...] = _dot(xb, wv_ref[...]).astype(BF16)
    u = _dot(xb, wu_ref[...])
    nb = u.shape[1] // 2
    h_ref[...] = u[:, :nb] * jax.nn.sigmoid(u[:, nb:])


def _even_in(x2, wq, wk, wv, wu, cos_t, sa_t, sb_t, seq):
    T, D = x2.shape
    tm = min(ROW_TILE, seq)
    nseq = seq // tm
    row = lambda i: (i, 0)
    tab = pl.BlockSpec((tm, LANES), lambda i: (i % nseq, 0))
    return pl.pallas_call(
        _even_in_kernel,
        grid=(T // tm,),
        in_specs=[pl.BlockSpec((tm, D), row), _resident(wq.shape), _resident(wk.shape),
                  _resident(wv.shape), _resident(wu.shape), tab, tab, tab],
        out_specs=[pl.BlockSpec((tm, wq.shape[1]), row), pl.BlockSpec((tm, wk.shape[1]), row),
                   pl.BlockSpec((tm, wv.shape[1]), row), pl.BlockSpec((tm, wu.shape[1] // 2), row)],
        out_shape=[jax.ShapeDtypeStruct((T, wq.shape[1]), BF16), jax.ShapeDtypeStruct((T, wk.shape[1]), BF16),
                   jax.ShapeDtypeStruct((T, wv.shape[1]), BF16), jax.ShapeDtypeStruct((T, wu.shape[1] // 2), F32)],
        compiler_params=_cparams("parallel"),
        name="even_in",
    )(x2, wq, wk, wv, wu, cos_t, sa_t, sb_t)


def _attn_kernel(sink_ref, q_ref, kp_ref, kc_ref, vp_ref, vc_ref, o_ref):
    n = pl.program_id(1)
    W = A_WINDOW
    G = A_HEADS // A_KV_HEADS
    nsub = q_ref.shape[0] // W
    r = lax.broadcasted_iota(jnp.int32, (G * W, 2 * W), 0)
    i = jnp.bitwise_and(r, W - 1)
    c = lax.broadcasted_iota(jnp.int32, (G * W, 2 * W), 1)
    band = jnp.logical_and(c > i, c <= i + W)
    neg = jnp.float32(-jnp.inf)
    bias_inner = jnp.where(band, 0.0, neg)
    bias_first = jnp.where(jnp.logical_and(band, jnp.logical_or(c >= W, n > 0)), 0.0, neg)
    rg = lax.broadcasted_iota(jnp.int32, (G * W, 1), 0) // W
    lane = lax.broadcasted_iota(jnp.int32, (W, LANES), 1)
    lo = lane < A_HEAD_DIM
    ones = jnp.ones((2 * W, LANES), BF16)
    for sub in range(nsub):
        rows = slice(sub * W, (sub + 1) * W)
        prow = slice((sub - 1) * W, sub * W)
        bias = bias_first if sub == 0 else bias_inner
        for h in range(A_KV_HEADS):
            kl = slice(h * LANES, (h + 1) * LANES)
            kprev = kp_ref[:, kl] if sub == 0 else kc_ref[prow, kl]
            kcat = jnp.concatenate([kprev, kc_ref[rows, kl]], axis=0)
            parts = []
            for jj in range(G // 2):
                j = h * (G // 2) + jj
                q2 = q_ref[rows, j * LANES:(j + 1) * LANES]
                parts.append(jnp.where(lo, q2, jnp.zeros_like(q2)))
                parts.append(jnp.where(lo, jnp.zeros_like(q2), q2))
            s = _dot_nt(jnp.concatenate(parts, axis=0), kcat) + bias
            sink = jnp.full((G * W, 1), sink_ref[h * G], F32)
            for gi in range(1, G):
                sink = jnp.where(rg == gi, sink_ref[h * G + gi], sink)
            m = jnp.maximum(jnp.max(jnp.maximum(s[:, :W], s[:, W:]), axis=-1, keepdims=True), sink)
            p = jnp.exp(s - m).astype(BF16)
            esink = jnp.exp(sink - m)
            vaug = []
            for half in range(2):
                vl = slice((2 * h + half) * LANES, (2 * h + half + 1) * LANES)
                vprev = vp_ref[:, vl] if sub == 0 else vc_ref[prow, vl]
                vcat = jnp.concatenate([vprev, vc_ref[rows, vl]], axis=0)
                vaug.append(jnp.concatenate([vcat, ones], axis=1))
            for jj in range(G // 2):
                j = h * (G // 2) + jj
                he = slice((2 * jj) * W, (2 * jj + 1) * W)
                ho = slice((2 * jj + 1) * W, (2 * jj + 2) * W)
                re = _dot(p[he], vaug[0])
                ro = _dot(p[ho], vaug[1])
                acc = re[:, :LANES] + ro[:, :LANES]
                d = jnp.where(lo, re[:, LANES:] + esink[he], ro[:, LANES:] + esink[ho])
                o_ref[rows, j * LANES:(j + 1) * LANES] = (acc * (1.0 / d)).astype(BF16)


def _attention(sinks, q, k, v, batch, seq):
    W = A_WINDOW
    tq = min(ATTN_TILE, seq)
    per = tq // W
    nt = seq // tq
    cur = lambda b, n: (b * nt + n, 0)
    prev = lambda b, n: (jnp.maximum((b * nt + n) * per - 1, 0), 0)
    return pl.pallas_call(
        _attn_kernel,
        grid=(batch, nt),
        in_specs=[pl.BlockSpec(memory_space=pltpu.SMEM),
                  pl.BlockSpec((tq, q.shape[1]), cur),
                  pl.BlockSpec((W, k.shape[1]), prev), pl.BlockSpec((tq, k.shape[1]), cur),
                  pl.BlockSpec((W, v.shape[1]), prev), pl.BlockSpec((tq, v.shape[1]), cur)],
        out_specs=pl.BlockSpec((tq, q.shape[1]), cur),
        out_shape=jax.ShapeDtypeStruct(q.shape, BF16),
        compiler_params=_cparams("parallel", "parallel"),
        name="swa_attention",
    )(sinks, q, k, k, v, v)


def _conv_kernel(hp_ref, hc_ref, w_ref, b_ref, g_ref, beta_ref, o_ref, var_ref):
    n = pl.program_id(1)
    tc, C = hc_ref.shape
    ext_rows = CONV_HALO + tc
    for c0 in range(0, C, LANES):
        cols = slice(c0, c0 + LANES)
        ext = jnp.concatenate([jnp.where(n > 0, hp_ref[:, cols], 0.0), hc_ref[:, cols]], axis=0)
        var_ref[0, :, :, cols] = ext.reshape(ext_rows // SUBLANES, SUBLANES, LANES)
        for r in range(1, SUBLANES):
            sh = pltpu.roll(ext, ext_rows - r, 0)
            var_ref[r, :, :, cols] = sh.reshape(ext_rows // SUBLANES, SUBLANES, LANES)
    first = CONV_HALO - (B_CONV - 1)
    nt = CONV_ROWS // SUBLANES
    for r0 in range(0, tc, CONV_ROWS):
        acc = None
        for kk in range(B_CONV):
            d = first + kk
            s = r0 // SUBLANES + d // SUBLANES
            t = w_ref[kk] * var_ref[d % SUBLANES, s:s + nt]
            acc = t + b_ref[...] if acc is None else acc + t
        y = _layer_norm(acc, g_ref[...], beta_ref[...])
        o_ref[r0:r0 + CONV_ROWS, :] = _silu(y).reshape(CONV_ROWS, C).astype(BF16)


def _conv_module(h, w, b, g, beta, batch, seq):
    T, C = h.shape
    tc = min(CONV_TILE, seq)
    nt = seq // tc
    per = tc // CONV_HALO
    cur = lambda bi, n: (bi * nt + n, 0)
    prev = lambda bi, n: (jnp.maximum((bi * nt + n) * per - 1, 0), 0)
    return pl.pallas_call(
        _conv_kernel,
        grid=(batch, nt),
        in_specs=[pl.BlockSpec((CONV_HALO, C), prev), pl.BlockSpec((tc, C), cur),
                  _resident(w.shape), _resident(b.shape), _resident(g.shape), _resident(beta.shape)],
        out_specs=pl.BlockSpec((tc, C), cur),
        out_shape=jax.ShapeDtypeStruct((T, C), BF16),
        scratch_shapes=[pltpu.VMEM((SUBLANES, (CONV_HALO + tc) // SUBLANES, SUBLANES, C), F32)],
        compiler_params=_cparams("parallel", "parallel"),
        name="conv_module",
    )(h, h, w, b, g, beta)


def _out_ffn_kernel(alpha, ya_ref, yb_ref, x_ref, wa_ref, wb_ref, g1_ref, b1_ref,
                    wg_ref, wu_ref, wd_ref, g2_ref, b2_ref, o_ref, x1_ref, xb_ref):
    mix = _dot(ya_ref[...], wa_ref[...]) + _dot(yb_ref[...], wb_ref[...])
    x1 = _layer_norm(alpha * x_ref[...] + mix, g1_ref[...], b1_ref[...])
    x1_ref[...] = x1
    xb_ref[...] = x1.astype(BF16)
    hidden = wg_ref.shape[1]
    acc = None
    for c0 in range(0, hidden, FFN_CHUNK):
        c1 = min(c0 + FFN_CHUNK, hidden)
        gate = _dot(xb_ref[...], wg_ref[:, c0:c1])
        up = _dot(xb_ref[...], wu_ref[:, c0:c1])
        t = _dot((_silu(gate) * up).astype(BF16), wd_ref[c0:c1, :])
        acc = t if acc is None else acc + t
    o_ref[...] = _layer_norm(alpha * x1_ref[...] + acc, g2_ref[...], b2_ref[...])


def _out_ffn(ya, yb, x2, wa, wb, g1, b1, wg, wu, wd, g2, b2, alpha):
    T, D = x2.shape
    tm = min(ROW_TILE, T)
    row = lambda i: (i, 0)
    consts = (wa, wb, g1, b1, wg, wu, wd, g2, b2)
    return pl.pallas_call(
        functools.partial(_out_ffn_kernel, alpha),
        grid=(T // tm,),
        in_specs=[pl.BlockSpec((tm, ya.shape[1]), row), pl.BlockSpec((tm, yb.shape[1]), row),
                  pl.BlockSpec((tm, D), row)] + [_resident(t.shape) for t in consts],
        out_specs=pl.BlockSpec((tm, D), row),
        out_shape=jax.ShapeDtypeStruct((T, D), F32),
        scratch_shapes=[pltpu.VMEM((tm, D), F32), pltpu.VMEM((tm, D), BF16)],
        compiler_params=_cparams("parallel"),
        name="out_ffn",
    )(ya, yb, x2, *consts)


def _odd_in_kernel(nseq, x_ref, wz_ref, wx_ref, wdt_ref, wq_ref, wk_ref, wv_ref, wg_ref, cos_ref, sin_ref,
                   cw_ref, cb_ref, dtb_ref,
                   z_ref, xs_ref, bc_ref, dt_ref, q_ref, k_ref, v_ref, g_ref, buf_ref):
    i = pl.program_id(0)
    tm = x_ref.shape[0]
    inner = xs_ref.shape[1]
    xb = x_ref[...].astype(BF16)

    def col_blocks(w_ref):
        for c0 in range(0, w_ref.shape[1], PROJ_COLS):
            c1 = min(c0 + PROJ_COLS, w_ref.shape[1])
            yield c0, c1, _dot(xb, w_ref[:, c0:c1])

    @pl.when(i % nseq == 0)
    def _():
        buf_ref[0:SUBLANES, :] = jnp.zeros((SUBLANES, buf_ref.shape[1]), F32)

    @pl.when(i % nseq != 0)
    def _():
        buf_ref[0:SUBLANES, :] = buf_ref[tm:tm + SUBLANES, :]

    rblk = min(CONV4_ROWS, tm)
    ntile = (SUBLANES + rblk) // SUBLANES

    def conv_block(r0, c0):
        cols = slice(c0, c0 + LANES)
        ext = buf_ref[r0:r0 + SUBLANES + rblk, cols]
        acc = cb_ref[:, cols] + cw_ref[C_CONV - 1, :, cols] * ext.reshape(ntile, SUBLANES, LANES)
        for kk in range(1, C_CONV):
            sh = pltpu.roll(ext, kk, 0).reshape(ntile, SUBLANES, LANES)
            acc = acc + cw_ref[C_CONV - 1 - kk, :, cols] * sh
        y = _silu(acc[1:]).reshape(rblk, LANES)
        if c0 < inner:
            xs_ref[r0:r0 + rblk, cols] = y
        else:
            bc_ref[r0:r0 + rblk, c0 - inner:c0 - inner + LANES] = y.astype(BF16)

    for c0, c1, r in col_blocks(wx_ref):
        buf_ref[SUBLANES:SUBLANES + tm, c0:c1] = r
        for cc in range(c0, c1, LANES):
            for r0 in range(0, tm, rblk):
                conv_block(r0, cc)

    for c0, c1, r in col_blocks(wz_ref):
        z_ref[:, c0:c1] = _silu(r)

    c, s = cos_ref[...], sin_ref[...]
    kscale = D_QK_DIM ** -0.5

    def rot(t):
        return t * c + pltpu.roll(t, D_QK_DIM // 2, 1) * s

    for c0, c1, r in col_blocks(wq_ref):
        for cc in range(c0, c1, D_QK_DIM):
            q_ref[:, cc:cc + D_QK_DIM] = rot(r[:, cc - c0:cc - c0 + D_QK_DIM]).astype(BF16)
    for c0, c1, r in col_blocks(wk_ref):
        for cc in range(c0, c1, D_QK_DIM):
            k_ref[:, cc:cc + D_QK_DIM] = rot(r[:, cc - c0:cc - c0 + D_QK_DIM]) * kscale
    for c0, c1, r in col_blocks(wv_ref):
        v_ref[:, c0:c1] = r.astype(BF16)
    for c0, c1, r in col_blocks(wg_ref):
        g_ref[:, c0:c1] = _silu(r)
    dtv = _dot(xb, wdt_ref[...]) + dtb_ref[...]
    dt_ref[...] = jnp.maximum(dtv, 0.0) + jnp.log1p(jnp.exp(-jnp.abs(dtv)))


def _odd_in(x2, wz, wx, wdt, wq, wk, wv, wg, cos_t, sin_t, cw, cb, dtb, seq):
    T, D = x2.shape
    tm = min(ROW_TILE, seq)
    nseq = seq // tm
    inner = wz.shape[1]
    row = lambda i: (i, 0)
    tab = pl.BlockSpec((tm, LANES), lambda i: (i % nseq, 0))
    ws = (wz, wx, wdt, wq, wk, wv, wg)
    consts = (cw, cb, dtb)
    widths = (inner, inner, wx.shape[1] - inner, wdt.shape[1], wq.shape[1], wk.shape[1], wv.shape[1], wg.shape[1])
    dts = (F32, F32, BF16, F32, BF16, F32, BF16, F32)
    return pl.pallas_call(
        functools.partial(_odd_in_kernel, nseq),
        grid=(T // tm,),
        in_specs=[pl.BlockSpec((tm, D), row)] + [_resident(w.shape) for w in ws] + [tab, tab]
                 + [_resident(t.shape) for t in consts],
        out_specs=[pl.BlockSpec((tm, n), row) for n in widths],
        out_shape=[jax.ShapeDtypeStruct((T, n), dt) for n, dt in zip(widths, dts)],
        scratch_shapes=[pltpu.VMEM((SUBLANES + tm, wx.shape[1]), F32)],
        compiler_params=_cparams("arbitrary"),
        name="odd_in",
    )(x2, *ws, cos_t, sin_t, *consts)


def _ssd_kernel(z_ref, xs_ref, bc_ref, dt_ref, alog_ref, dskip_ref, ng_ref,
                expand_ref, tri_ref, o_ref, state_ref, y_ref):
    @pl.when(pl.program_id(1) == 0)
    def _():
        state_ref[...] = jnp.zeros_like(state_ref)

    L = C_CHUNK
    nchunk = z_ref.shape[0] // L
    dt = dt_ref[...]
    dta = dt * (-jnp.exp(alog_ref[...]))
    acs_c = [_cumsum_rows(dta[c * L:(c + 1) * L], tri_ref[...]) for c in range(nchunk)]
    acs = jnp.concatenate(acs_c, axis=0)
    last = jnp.concatenate([jnp.broadcast_to(t[L - 1:L, :], t.shape) for t in acs_c], axis=0)
    to_state_e = _expand_heads(dt * jnp.exp(last - acs), expand_ref[...])
    from_start_e = _expand_heads(jnp.exp(acs), expand_ref[...])
    for c in range(nchunk):
        rows = slice(c * L, (c + 1) * L)
        _ssd_chunk(z_ref.at[rows], xs_ref.at[rows], bc_ref.at[rows], acs_c[c], dt[rows], to_state_e[rows],
                   from_start_e[rows], dskip_ref, ng_ref, o_ref.at[rows], state_ref, y_ref)


def _ssd_chunk(z_ref, xs_ref, bc_ref, acs, dt, to_state_e, from_start_e, dskip_ref, ng_ref,
               o_ref, state_ref, y_ref):
    L = C_CHUNK
    gw = C_STATE
    hpg = C_HEADS // C_GROUPS
    gcols = hpg * C_HEAD_DIM
    xs = xs_ref[...]
    xs_b = xs.astype(BF16)
    acs_t = acs.T
    dt_t = dt.T
    cdec_e = from_start_e[L - 1:L, :]
    wst = (xs * to_state_e).astype(BF16)

    row = lax.broadcasted_iota(jnp.int32, (L, L), 0)
    col = lax.broadcasted_iota(jnp.int32, (L, L), 1)
    causal = col <= row
    lane = lax.broadcasted_iota(jnp.int32, (L, LANES), 1)
    lo = lane < C_HEAD_DIM
    zero_b = jnp.zeros((L, LANES), BF16)
    quad = 2 * LANES

    for g in range(C_GROUPS):
        bm = bc_ref[:, g * gw:(g + 1) * gw]
        cm = bc_ref[:, (C_GROUPS + g) * gw:(C_GROUPS + g + 1) * gw]
        cbm = _dot_nt(cm, bm)
        gs = slice(g * gcols, (g + 1) * gcols)
        prev = state_ref[:, gs]
        y_off = _dot(cm, prev.astype(BF16)) * from_start_e[:, gs]
        new = _dot_tn(bm, wst[:, gs])
        state_ref[:, gs] = prev * cdec_e[:, gs] + new
        for qd in range(gcols // quad):
            ms, xr = [], []
            for jp in range(2):
                c0 = g * gcols + qd * quad + jp * LANES
                xp = xs_b[:, c0:c0 + LANES]
                for half in range(2):
                    h = (c0 // C_HEAD_DIM) + half
                    seg = acs[:, h:h + 1] - acs_t[h:h + 1, :]
                    dec = jnp.exp(jnp.where(causal, seg, -jnp.inf))
                    ms.append((cbm * dec * dt_t[h:h + 1, :]).astype(BF16))
                    xh = jnp.where(lo if half == 0 else jnp.logical_not(lo), xp, zero_b)
                    xr.append(jnp.concatenate([xh, zero_b] if jp == 0 else [zero_b, xh], axis=1))
            cols = slice(g * gcols + qd * quad, g * gcols + (qd + 1) * quad)
            y = (_dot(jnp.concatenate(ms, axis=1), jnp.concatenate(xr, axis=0))
                 + y_off[:, qd * quad:(qd + 1) * quad] + xs[:, cols] * dskip_ref[:, cols])
            y_ref[:, cols] = y * z_ref[:, cols]

    for g in range(C_GROUPS):
        gs = slice(g * gcols, (g + 1) * gcols)
        yg = y_ref[:, gs]
        ms = jnp.mean(yg * yg, axis=-1, keepdims=True)
        o_ref[:, gs] = (yg * lax.rsqrt(ms + NORM_EPS) * ng_ref[:, gs]).astype(BF16)


def _expand_heads(v, expand3):
    lane = lax.broadcasted_iota(jnp.int32, v.shape, 1)
    hi, mid, low = _split_bf16(jnp.where(lane < C_HEADS, v, 0.0), 3)
    packed = (hi.astype(F32) + pltpu.roll(mid.astype(F32), C_HEADS, 1)
              + pltpu.roll(low.astype(F32), 2 * C_HEADS, 1))
    return _dot(packed.astype(BF16), expand3)


def _cumsum_rows(v, tri):
    acc = None
    for p in _split_bf16(v, 3):
        t = _dot(tri, p)
        acc = t if acc is None else acc + t
    return acc


def _ssd(z, xs, bc, dt, alog, dskip, ng, expand, tri, batch, seq):
    tr = min(SCAN_TILE, seq)
    nt = seq // tr
    cur = lambda b, c: (b * nt + c, 0)
    consts = (alog, dskip, ng, expand, tri)
    return pl.pallas_call(
        _ssd_kernel,
        grid=(batch, nt),
        in_specs=[pl.BlockSpec((tr, t.shape[1]), cur) for t in (z, xs, bc, dt)]
                 + [_resident(t.shape) for t in consts],
        out_specs=pl.BlockSpec((tr, z.shape[1]), cur),
        out_shape=jax.ShapeDtypeStruct(z.shape, BF16),
        scratch_shapes=[pltpu.VMEM((C_STATE, z.shape[1]), F32),
                        pltpu.VMEM((C_CHUNK, z.shape[1]), F32)],
        compiler_params=_cparams("arbitrary", "arbitrary"),
        name="ssd_scan",
    )(z, xs, bc, dt, *consts)


def _ret_kernel(q_ref, k_ref, v_ref, g_ref, dmat_ref, toend_ref, fstart_ref, cdec_ref, gng_ref, gnb_ref,
                o_ref, state_ref):
    @pl.when(pl.program_id(1) == 0)
    def _():
        state_ref[...] = jnp.zeros_like(state_ref)

    L = D_CHUNK
    for r0 in range(0, q_ref.shape[0], L):
        rows = slice(r0, r0 + L)
        for h in range(D_HEADS):
            ks = slice(h * D_QK_DIM, (h + 1) * D_QK_DIM)
            vs = slice(h * D_V_DIM, (h + 1) * D_V_DIM)
            qh = q_ref[rows, ks]
            kh = k_ref[rows, ks]
            vh = v_ref[rows, vs]
            scores = _dot_nt(qh, kh.astype(BF16)) * dmat_ref[h]
            prev = state_ref[:, vs]
            y = _dot(scores.astype(BF16), vh) + _dot(qh, prev.astype(BF16)) * fstart_ref[:, vs]
            kd = (kh * toend_ref[:, ks]).astype(BF16)
            state_ref[:, vs] = prev * cdec_ref[:, vs] + _dot_tn(kd, vh)
            mu = jnp.mean(y, axis=-1, keepdims=True)
            d = y - mu
            var = jnp.mean(d * d, axis=-1, keepdims=True)
            yn = d * lax.rsqrt(var + NORM_EPS) * gng_ref[:, vs] + gnb_ref[:, vs]
            o_ref[rows, vs] = (g_ref[rows, vs] * yn).astype(BF16)


def _retention(q, k, v, g, dmat, toend, fstart, cdec, gng, gnb, batch, seq):
    L = min(SCAN_TILE, seq)
    nc = seq // L
    cur = lambda b, c: (b * nc + c, 0)
    consts = (dmat, toend, fstart, cdec, gng, gnb)
    return pl.pallas_call(
        _ret_kernel,
        grid=(batch, nc),
        in_specs=[pl.BlockSpec((L, q.shape[1]), cur), pl.BlockSpec((L, k.shape[1]), cur),
                  pl.BlockSpec((L, v.shape[1]), cur), pl.BlockSpec((L, g.shape[1]), cur)]
                 + [_resident(t.shape) for t in consts],
        out_specs=pl.BlockSpec((L, v.shape[1]), cur),
        out_shape=jax.ShapeDtypeStruct(v.shape, BF16),
        scratch_shapes=[pltpu.VMEM((D_QK_DIM, v.shape[1]), F32)],
        compiler_params=_cparams("arbitrary", "arbitrary"),
        name="retention_scan",
    )(q, k, v, g, *consts)


def _even_tables(seq):
    half = A_ROPE_DIM // 2
    inv = jnp.power(jnp.float32(ROPE_THETA), -jnp.arange(half, dtype=F32) / half)
    ang = jnp.arange(seq, dtype=F32)[:, None] * inv[None, :]
    cos, sin = jnp.cos(ang), jnp.sin(ang)
    ones = jnp.ones((seq, A_HEAD_DIM - 2 * half), F32)
    zeros = jnp.zeros((seq, A_HEAD_DIM - 2 * half), F32)
    zh = jnp.zeros((seq, half), F32)
    c = jnp.concatenate([cos, cos, ones], axis=1)
    sa = jnp.concatenate([zh, sin, zeros], axis=1)
    sb = jnp.concatenate([-sin, zh, zeros], axis=1)
    rep = LANES // A_HEAD_DIM
    return jnp.tile(c, (1, rep)), jnp.tile(sa, (1, rep)), jnp.tile(sb, (1, rep))


def _ret_tables(seq):
    half = D_QK_DIM // 2
    inv = 1.0 / jnp.power(jnp.float32(RET_THETA), jnp.linspace(0.0, 1.0, half, dtype=F32))
    ang = jnp.arange(seq, dtype=F32)[:, None] * inv[None, :]
    cos, sin = jnp.cos(ang), jnp.sin(ang)
    L = D_CHUNK
    log_gamma = jnp.log(1.0 - jnp.power(2.0, -5.0 - jnp.arange(D_HEADS, dtype=F32)))
    idx = jnp.arange(L, dtype=F32)
    rel = idx[:, None] - idx[None, :]
    dmat = jnp.where(rel[None] >= 0, jnp.exp(log_gamma[:, None, None] * jnp.maximum(rel, 0.0)[None]), 0.0)
    to_end = jnp.exp(log_gamma[None, :] * (L - 1 - idx)[:, None])
    from_start = jnp.exp(log_gamma[None, :] * (idx + 1.0)[:, None])
    chunk_decay = jnp.exp(log_gamma * L)
    return (jnp.concatenate([cos, cos], axis=1), jnp.concatenate([-sin, sin], axis=1), dmat,
            jnp.repeat(to_end, D_QK_DIM, axis=1), jnp.repeat(from_start, D_V_DIM, axis=1),
            jnp.repeat(chunk_decay, D_V_DIM)[None, :])


def _row(v):
    return v.astype(F32)[None, :]


def _sublane_rep(v):
    v = v.astype(F32)
    return jnp.broadcast_to(v[..., None, :], v.shape[:-1] + (SUBLANES, v.shape[-1]))


def _pad_lanes(v, fill=0.0):
    pad = (-v.shape[-1]) % LANES
    return jnp.pad(v, [(0, 0)] * (v.ndim - 1) + [(0, pad)], constant_values=fill)


def _even_mixer(x2, w_in, sinks, dw_w, dw_b, cn_g, cn_b, w_out, tables, batch, seq):
    nq = A_HEADS * A_HEAD_DIM
    nkv = A_KV_HEADS * A_HEAD_DIM
    wq = w_in[:, :nq].astype(BF16)
    wk = w_in[:, nq:nq + nkv].astype(BF16)
    wv = w_in[:, nq + nkv:nq + 2 * nkv].astype(BF16)
    wu = w_in[:, nq + 2 * nkv:].astype(BF16)
    D = w_in.shape[0]
    wk = jnp.repeat(wk.reshape(D, A_KV_HEADS, 1, A_HEAD_DIM), 2, axis=2).reshape(D, 2 * nkv)
    wv4 = wv.reshape(D, A_KV_HEADS, 1, A_HEAD_DIM)
    zv = jnp.zeros_like(wv4)
    wv = jnp.concatenate([wv4, zv, zv, wv4], axis=2).reshape(D, 4 * nkv)
    q, k, v, hglu = _even_in(x2, wq, wk, wv, wu, *tables, seq)
    y_attn = _attention(sinks.astype(F32), q, k, v, batch, seq)
    y_conv = _conv_module(hglu, _sublane_rep(dw_w), _sublane_rep(dw_b), _sublane_rep(cn_g), _sublane_rep(cn_b),
                          batch, seq)
    wo = w_out.astype(BF16)
    return y_attn, y_conv, wo[:nq], wo[nq:]


def _odd_mixer(x2, w_in, conv_w, conv_b, dt_bias, a_log, d_skip, ssm_norm_g, gn_g, gn_b, w_out,
               tables, consts, batch, seq):
    inner = C_HEADS * C_HEAD_DIM
    xbc_w = inner + 2 * C_GROUPS * C_STATE
    nqk = D_HEADS * D_QK_DIM
    nv = D_HEADS * D_V_DIM
    sizes = (inner, xbc_w, C_HEADS, nqk, nqk, nv, nv)
    offs = [0]
    for s in sizes:
        offs.append(offs[-1] + s)
    wb = w_in.astype(BF16)
    wz, wx, wdt, wq, wk, wv, wg = [wb[:, offs[i]:offs[i + 1]] for i in range(len(sizes))]
    wdt = _pad_lanes(wdt)
    cos_t, sin_t, dmat, toend, fstart, cdec = tables
    expand, tri = consts
    z, xs, bc, dt, rq, rk, rv, rg = _odd_in(x2, wz, wx, wdt, wq, wk, wv, wg, cos_t, sin_t,
                                            _sublane_rep(conv_w), _sublane_rep(conv_b),
                                            _pad_lanes(_row(dt_bias)), seq)
    y_ssm = _ssd(z, xs, bc, dt, _pad_lanes(_row(a_log)), _row(jnp.repeat(d_skip.astype(F32), C_HEAD_DIM)),
                 _row(ssm_norm_g), expand, tri, batch, seq)
    y_ret = _retention(rq, rk, rv, rg, dmat, toend, fstart, cdec, _row(gn_g), _row(gn_b), batch, seq)
    wo = w_out.astype(BF16)
    return y_ssm, y_ret, wo[:inner], wo[inner:]


def _ssd_consts():
    k = jnp.arange(LANES)[:, None]
    cols = jnp.arange(C_HEADS * C_HEAD_DIM)[None, :] // C_HEAD_DIM
    expand = jnp.logical_and(k % C_HEADS == cols, k < 3 * C_HEADS).astype(BF16)
    i = jnp.arange(C_CHUNK)
    tri = (i[None, :] <= i[:, None]).astype(BF16)
    return expand, tri


def kernel(x, ln_mix_g, ln_mix_b, ln_ffn_g, ln_ffn_b, ffn_w_gate, ffn_w_up, ffn_w_down, ev_w_in, ev_sinks, ev_dw_w, ev_dw_b, ev_cn_g, ev_cn_b, ev_w_out, od_w_in, od_conv_w, od_conv_b, od_dt_bias, od_a_log, od_d_skip, od_ssm_norm_g, od_ret_gn_g, od_ret_gn_b, od_w_out):
    batch, seq, d_model = x.shape
    depth = ln_mix_g.shape[0]
    alpha = float((2 * depth) ** 0.25)
    ev_tables = _even_tables(seq)
    od_tables = _ret_tables(seq)
    od_consts = _ssd_consts()
    x2 = x.reshape(batch * seq, d_model)
    for layer in range(depth):
        i = layer // 2
        if layer % 2 == 0:
            mixed = _even_mixer(x2, ev_w_in[i], ev_sinks[i], ev_dw_w[i], ev_dw_b[i], ev_cn_g[i], ev_cn_b[i],
                                ev_w_out[i], ev_tables, batch, seq)
        else:
            mixed = _odd_mixer(x2, od_w_in[i], od_conv_w[i], od_conv_b[i], od_dt_bias[i], od_a_log[i],
                               od_d_skip[i], od_ssm_norm_g[i], od_ret_gn_g[i], od_ret_gn_b[i], od_w_out[i],
                               od_tables, od_consts, batch, seq)
        x2 = _mixer_out_ffn(mixed, x2, ln_mix_g[layer], ln_mix_b[layer], ffn_w_gate[layer], ffn_w_up[layer],
                            ffn_w_down[layer], ln_ffn_g[layer], ln_ffn_b[layer], alpha)
    return x2.reshape(batch, seq, d_model)


def _mixer_out_ffn(mixed, x2, ln1_g, ln1_b, w_gate, w_up, w_down, ln2_g, ln2_b, alpha):
    ya, yb, wa, wb = mixed
    return _out_ffn(ya, yb, x2, wa, wb, _row(ln1_g), _row(ln1_b), w_gate.astype(BF16), w_up.astype(BF16),
                    w_down.astype(BF16), _row(ln2_g), _row(ln2_b), alpha)
```

```python
import functools
import math

import jax
import jax.numpy as jnp
from jax import lax
from jax.experimental import pallas as pl
from jax.experimental.pallas import tpu as pltpu

F32 = jnp.float32
BF16 = jnp.bfloat16

A_HEADS = 8
A_KV_HEADS = 2
A_HEAD_DIM = 64
A_WINDOW = 128
ROPE_THETA = 500000.0
A_ROPE_DIM = A_HEAD_DIM // 4
B_CONV = 31
C_HEADS = 16
C_HEAD_DIM = 64
C_GROUPS = 2
C_STATE = 128
C_CONV = 4
C_CHUNK = 128
D_HEADS = 4
D_QK_DIM = 128
D_V_DIM = 256
D_CHUNK = 128
RET_THETA = 10000.0
NORM_EPS = 1e-5

LANES = 128
SUBLANES = 8
VMEM_LIMIT = 56 * 1024 * 1024

ROW_TILE = 512
ATTN_TILE = 512
CONV_TILE = 512
CONV_ROWS = 32
CONV_HALO = 32
FFN_CHUNK = 256
PROJ_COLS = 256
CONV4_ROWS = 256
SCAN_TILE = 512


def _cparams(*sem):
    return pltpu.CompilerParams(dimension_semantics=sem, vmem_limit_bytes=VMEM_LIMIT)


def _resident(shape):
    nd = len(shape)
    return pl.BlockSpec(shape, lambda *_: (0,) * nd, pipeline_mode=pl.Buffered(1))


def _resident_layer(shape, layer):
    return pl.BlockSpec((None,) + tuple(shape[1:]), lambda *_: (layer, 0, 0), pipeline_mode=pl.Buffered(1))


def _layer_norm(y, g, b):
    mu = jnp.mean(y, axis=-1, keepdims=True)
    d = y - mu
    var = jnp.mean(d * d, axis=-1, keepdims=True)
    return d * lax.rsqrt(var + NORM_EPS) * g + b


def _silu(v):
    h = 0.5 * v
    return h + h * jnp.tanh(h)


def _dot(a, b):
    return jnp.dot(a, b, preferred_element_type=F32)


def _dot_nt(a, b):
    return lax.dot_general(a, b, (((1,), (1,)), ((), ())), preferred_element_type=F32)


def _dot_tn(a, b):
    return lax.dot_general(a, b, (((0,), (0,)), ((), ())), preferred_element_type=F32)


def _split_bf16(v, parts):
    out = []
    r = v
    for _ in range(parts):
        p = r.astype(BF16)
        out.append(p)
        r = r - p.astype(F32)
    return out


def _even_in_kernel(x_ref, wq_ref, wkv_ref, wu_ref, cos_ref, sa_ref, sb_ref,
                    q_ref, k_ref, v_ref, h_ref):
    xb = x_ref[...].astype(BF16)
    c, sa, sb = cos_ref[...], sa_ref[...], sb_ref[...]
    rot_dim = A_ROPE_DIM // 2

    def rot(t):
        return t * c + pltpu.roll(t, rot_dim, 1) * sa + pltpu.roll(t, LANES - rot_dim, 1) * sb

    qscale = A_HEAD_DIM ** -0.5
    q = _dot(xb, wq_ref[...])
    for j in range(q.shape[1] // LANES):
        q_ref[:, j * LANES:(j + 1) * LANES] = (rot(q[:, j * LANES:(j + 1) * LANES]) * qscale).astype(BF16)
    kv = _dot(xb, wkv_ref[...])
    k2 = rot(kv[:, :LANES])
    v2 = kv[:, LANES:]
    k2s = pltpu.roll(k2, A_HEAD_DIM, 1)
    v2s = pltpu.roll(v2, A_HEAD_DIM, 1)
    lo = lax.broadcasted_iota(jnp.int32, k2.shape, 1) < A_HEAD_DIM
    zero = jnp.zeros_like(v2)
    k_ref[:, :LANES] = jnp.where(lo, k2, k2s).astype(BF16)
    k_ref[:, LANES:] = jnp.where(lo, k2s, k2).astype(BF16)
    for j, blk in enumerate((jnp.where(lo, v2, zero), jnp.where(lo, zero, v2s),
                             jnp.where(lo, v2s, zero), jnp.where(lo, zero, v2))):
        v_ref[:, j * LANES:(j + 1) * LANES] = blk.astype(BF16)
    u = _dot(xb, wu_ref[...])
    nb = u.shape[1] // 2
    h_ref[...] = u[:, :nb] * jax.nn.sigmoid(u[:, nb:])


def _even_in(x2, wq, wkv, wu, cos_t, sa_t, sb_t, seq):
    T, D = x2.shape
    assert wkv.shape[1] == 2 * LANES, "k and v must fill exactly 128 lanes each"
    tm = min(ROW_TILE, seq)
    nseq = seq // tm
    row = lambda i: (i, 0)
    tab = pl.BlockSpec((tm, LANES), lambda i: (i % nseq, 0))
    widths = (wq.shape[1], 2 * LANES, 4 * LANES, wu.shape[1] // 2)
    dts = (BF16, BF16, BF16, F32)
    return pl.pallas_call(
        _even_in_kernel,
        grid=(T // tm,),
        in_specs=[pl.BlockSpec((tm, D), row), _resident(wq.shape), _resident(wkv.shape), _resident(wu.shape),
                  tab, tab, tab],
        out_specs=[pl.BlockSpec((tm, n), row) for n in widths],
        out_shape=[jax.ShapeDtypeStruct((T, n), dt) for n, dt in zip(widths, dts)],
        compiler_params=_cparams("parallel"),
        name="even_in",
    )(x2, wq, wkv, wu, cos_t, sa_t, sb_t)


def _attn_kernel(sink_ref, q_ref, kp_ref, kc_ref, vp_ref, vc_ref, o_ref):
    n = pl.program_id(1)
    W = A_WINDOW
    G = A_HEADS // A_KV_HEADS
    nsub = q_ref.shape[0] // W
    r = lax.broadcasted_iota(jnp.int32, (G * W, 2 * W), 0)
    i = jnp.bitwise_and(r, W - 1)
    c = lax.broadcasted_iota(jnp.int32, (G * W, 2 * W), 1)
    band = jnp.logical_and(c > i, c <= i + W)
    neg = jnp.float32(-jnp.inf)
    bias_inner = jnp.where(band, 0.0, neg)
    bias_first = jnp.where(jnp.logical_and(band, jnp.logical_or(c >= W, n > 0)), 0.0, neg)
    rg = lax.broadcasted_iota(jnp.int32, (G * W, 1), 0) // W
    lane = lax.broadcasted_iota(jnp.int32, (W, LANES), 1)
    lo = lane < A_HEAD_DIM
    ones = jnp.ones((2 * W, LANES), BF16)
    for sub in range(nsub):
        rows = slice(sub * W, (sub + 1) * W)
        prow = slice((sub - 1) * W, sub * W)
        bias = bias_first if sub == 0 else bias_inner
        for h in range(A_KV_HEADS):
            kl = slice(h * LANES, (h + 1) * LANES)
            kprev = kp_ref[:, kl] if sub == 0 else kc_ref[prow, kl]
            kcat = jnp.concatenate([kprev, kc_ref[rows, kl]], axis=0)
            parts = []
            for jj in range(G // 2):
                j = h * (G // 2) + jj
                q2 = q_ref[rows, j * LANES:(j + 1) * LANES]
                parts.append(jnp.where(lo, q2, jnp.zeros_like(q2)))
                parts.append(jnp.where(lo, jnp.zeros_like(q2), q2))
            s = _dot_nt(jnp.concatenate(parts, axis=0), kcat) + bias
            sink = jnp.full((G * W, 1), sink_ref[h * G], F32)
            for gi in range(1, G):
                sink = jnp.where(rg == gi, sink_ref[h * G + gi], sink)
            m = jnp.maximum(jnp.max(jnp.maximum(s[:, :W], s[:, W:]), axis=-1, keepdims=True), sink)
            p = jnp.exp(s - m).astype(BF16)
            esink = jnp.exp(sink - m)
            vaug = []
            for half in range(2):
                vl = slice((2 * h + half) * LANES, (2 * h + half + 1) * LANES)
                vprev = vp_ref[:, vl] if sub == 0 else vc_ref[prow, vl]
                vcat = jnp.concatenate([vprev, vc_ref[rows, vl]], axis=0)
                vaug.append(jnp.concatenate([vcat, ones], axis=1))
            for jj in range(G // 2):
                j = h * (G // 2) + jj
                he = slice((2 * jj) * W, (2 * jj + 1) * W)
                ho = slice((2 * jj + 1) * W, (2 * jj + 2) * W)
                re = _dot(p[he], vaug[0])
                ro = _dot(p[ho], vaug[1])
                acc = re[:, :LANES] + ro[:, :LANES]
                d = jnp.where(lo, re[:, LANES:] + esink[he], ro[:, LANES:] + esink[ho])
                o_ref[rows, j * LANES:(j + 1) * LANES] = (acc * (1.0 / d)).astype(BF16)


def _attention(sinks, q, k, v, batch, seq):
    W = A_WINDOW
    tq = min(ATTN_TILE, seq)
    per = tq // W
    nt = seq // tq
    cur = lambda b, n: (b * nt + n, 0)
    prev = lambda b, n: (jnp.maximum((b * nt + n) * per - 1, 0), 0)
    return pl.pallas_call(
        _attn_kernel,
        grid=(batch, nt),
        in_specs=[pl.BlockSpec(memory_space=pltpu.SMEM),
                  pl.BlockSpec((tq, q.shape[1]), cur),
                  pl.BlockSpec((W, k.shape[1]), prev), pl.BlockSpec((tq, k.shape[1]), cur),
                  pl.BlockSpec((W, v.shape[1]), prev), pl.BlockSpec((tq, v.shape[1]), cur)],
        out_specs=pl.BlockSpec((tq, q.shape[1]), cur),
        out_shape=jax.ShapeDtypeStruct(q.shape, BF16),
        compiler_params=_cparams("parallel", "parallel"),
        name="swa_attention",
    )(sinks, q, k, k, v, v)


def _conv_kernel(hp_ref, hc_ref, w_ref, b_ref, g_ref, beta_ref, o_ref, var_ref):
    n = pl.program_id(1)
    tc, C = hc_ref.shape
    ext_rows = CONV_HALO + tc
    for c0 in range(0, C, LANES):
        cols = slice(c0, c0 + LANES)
        ext = jnp.concatenate([jnp.where(n > 0, hp_ref[:, cols], 0.0), hc_ref[:, cols]], axis=0)
        var_ref[0, :, :, cols] = ext.reshape(ext_rows // SUBLANES, SUBLANES, LANES)
        for r in range(1, SUBLANES):
            sh = pltpu.roll(ext, ext_rows - r, 0)
            var_ref[r, :, :, cols] = sh.reshape(ext_rows // SUBLANES, SUBLANES, LANES)
    first = CONV_HALO - (B_CONV - 1)
    nt = CONV_ROWS // SUBLANES
    for r0 in range(0, tc, CONV_ROWS):
        acc = None
        for kk in range(B_CONV):
            d = first + kk
            s = r0 // SUBLANES + d // SUBLANES
            t = w_ref[kk] * var_ref[d % SUBLANES, s:s + nt]
            acc = t + b_ref[...] if acc is None else acc + t
        y = _layer_norm(acc, g_ref[...], beta_ref[...])
        o_ref[r0:r0 + CONV_ROWS, :] = _silu(y).reshape(CONV_ROWS, C).astype(BF16)


def _conv_module(h, w, b, g, beta, batch, seq):
    T, C = h.shape
    tc = min(CONV_TILE, seq)
    nt = seq // tc
    per = tc // CONV_HALO
    cur = lambda bi, n: (bi * nt + n, 0)
    prev = lambda bi, n: (jnp.maximum((bi * nt + n) * per - 1, 0), 0)
    return pl.pallas_call(
        _conv_kernel,
        grid=(batch, nt),
        in_specs=[pl.BlockSpec((CONV_HALO, C), prev), pl.BlockSpec((tc, C), cur),
                  _resident(w.shape), _resident(b.shape), _resident(g.shape), _resident(beta.shape)],
        out_specs=pl.BlockSpec((tc, C), cur),
        out_shape=jax.ShapeDtypeStruct((T, C), BF16),
        scratch_shapes=[pltpu.VMEM((SUBLANES, (CONV_HALO + tc) // SUBLANES, SUBLANES, C), F32)],
        compiler_params=_cparams("parallel", "parallel"),
        name="conv_module",
    )(h, h, w, b, g, beta)


def _out_ffn_kernel(alpha, ya_ref, yb_ref, x_ref, wa_ref, wb_ref, g1_ref, b1_ref,
                    wg_ref, wu_ref, wd_ref, g2_ref, b2_ref, o_ref, x1_ref, xb_ref):
    mix = _dot(ya_ref[...], wa_ref[...]) + _dot(yb_ref[...], wb_ref[...])
    x1 = _layer_norm(alpha * x_ref[...] + mix, g1_ref[...], b1_ref[...])
    x1_ref[...] = x1
    xb_ref[...] = x1.astype(BF16)
    hidden = wg_ref.shape[1]
    acc = None
    for c0 in range(0, hidden, FFN_CHUNK):
        c1 = min(c0 + FFN_CHUNK, hidden)
        gate = _dot(xb_ref[...], wg_ref[:, c0:c1])
        up = _dot(xb_ref[...], wu_ref[:, c0:c1])
        t = _dot((_silu(gate) * up).astype(BF16), wd_ref[c0:c1, :])
        acc = t if acc is None else acc + t
    o_ref[...] = _layer_norm(alpha * x1_ref[...] + acc, g2_ref[...], b2_ref[...])


def _out_ffn(ya, yb, x2, wa, wb, g1, b1, wg, wu, wd, g2, b2, layer, alpha):
    T, D = x2.shape
    tm = min(ROW_TILE, T)
    row = lambda i: (i, 0)
    consts = (wa, wb, g1, b1, wg, wu, wd, g2, b2)
    const_specs = [_resident_layer(t.shape, layer) if t.ndim == 3 else _resident(t.shape) for t in consts]
    return pl.pallas_call(
        functools.partial(_out_ffn_kernel, alpha),
        grid=(T // tm,),
        in_specs=[pl.BlockSpec((tm, ya.shape[1]), row), pl.BlockSpec((tm, yb.shape[1]), row),
                  pl.BlockSpec((tm, D), row)] + const_specs,
        out_specs=pl.BlockSpec((tm, D), row),
        out_shape=jax.ShapeDtypeStruct((T, D), F32),
        scratch_shapes=[pltpu.VMEM((tm, D), F32), pltpu.VMEM((tm, D), BF16)],
        compiler_params=_cparams("parallel"),
        name="out_ffn",
    )(ya, yb, x2, *consts)


def _odd_in_kernel(nseq, x_ref, wz_ref, wx_ref, wdt_ref, wq_ref, wk_ref, wv_ref, wg_ref, cos_ref, sin_ref,
                   cw_ref, cb_ref, dtb_ref,
                   z_ref, xs_ref, bc_ref, dt_ref, q_ref, k_ref, v_ref, g_ref, buf_ref):
    i = pl.program_id(0)
    tm = x_ref.shape[0]
    inner = xs_ref.shape[1]
    xb = x_ref[...].astype(BF16)

    def col_blocks(w_ref):
        for c0 in range(0, w_ref.shape[1], PROJ_COLS):
            c1 = min(c0 + PROJ_COLS, w_ref.shape[1])
            yield c0, c1, _dot(xb, w_ref[:, c0:c1])

    @pl.when(i % nseq == 0)
    def _():
        buf_ref[0:SUBLANES, :] = jnp.zeros((SUBLANES, buf_ref.shape[1]), F32)

    @pl.when(i % nseq != 0)
    def _():
        buf_ref[0:SUBLANES, :] = buf_ref[tm:tm + SUBLANES, :]

    rblk = min(CONV4_ROWS, tm)
    ntile = (SUBLANES + rblk) // SUBLANES

    def conv_block(r0, c0):
        cols = slice(c0, c0 + LANES)
        ext = buf_ref[r0:r0 + SUBLANES + rblk, cols]
        acc = cb_ref[:, cols] + cw_ref[C_CONV - 1, :, cols] * ext.reshape(ntile, SUBLANES, LANES)
        for kk in range(1, C_CONV):
            sh = pltpu.roll(ext, kk, 0).reshape(ntile, SUBLANES, LANES)
            acc = acc + cw_ref[C_CONV - 1 - kk, :, cols] * sh
        y = _silu(acc[1:]).reshape(rblk, LANES)
        if c0 < inner:
            xs_ref[r0:r0 + rblk, cols] = y
        else:
            bc_ref[r0:r0 + rblk, c0 - inner:c0 - inner + LANES] = y.astype(BF16)

    for c0, c1, r in col_blocks(wx_ref):
        buf_ref[SUBLANES:SUBLANES + tm, c0:c1] = r
        for cc in range(c0, c1, LANES):
            for r0 in range(0, tm, rblk):
                conv_block(r0, cc)

    for c0, c1, r in col_blocks(wz_ref):
        z_ref[:, c0:c1] = _silu(r)

    c, s = cos_ref[...], sin_ref[...]
    kscale = D_QK_DIM ** -0.5

    def rot(t):
        return t * c + pltpu.roll(t, D_QK_DIM // 2, 1) * s

    for c0, c1, r in col_blocks(wq_ref):
        for cc in range(c0, c1, D_QK_DIM):
            q_ref[:, cc:cc + D_QK_DIM] = rot(r[:, cc - c0:cc - c0 + D_QK_DIM]).astype(BF16)
    for c0, c1, r in col_blocks(wk_ref):
        for cc in range(c0, c1, D_QK_DIM):
            k_ref[:, cc:cc + D_QK_DIM] = rot(r[:, cc - c0:cc - c0 + D_QK_DIM]) * kscale
    for c0, c1, r in col_blocks(wv_ref):
        v_ref[:, c0:c1] = r.astype(BF16)
    for c0, c1, r in col_blocks(wg_ref):
        g_ref[:, c0:c1] = _silu(r)
    dtv = _dot(xb, wdt_ref[...]) + dtb_ref[...]
    dt_ref[...] = jnp.maximum(dtv, 0.0) + jnp.log1p(jnp.exp(-jnp.abs(dtv)))


def _odd_in(x2, wz, wx, wdt, wq, wk, wv, wg, cos_t, sin_t, cw, cb, dtb, seq):
    T, D = x2.shape
    tm = min(ROW_TILE, seq)
    nseq = seq // tm
    inner = wz.shape[1]
    row = lambda i: (i, 0)
    tab = pl.BlockSpec((tm, LANES), lambda i: (i % nseq, 0))
    ws = (wz, wx, wdt, wq, wk, wv, wg)
    consts = (cw, cb, dtb)
    widths = (inner, inner, wx.shape[1] - inner, wdt.shape[1], wq.shape[1], wk.shape[1], wv.shape[1], wg.shape[1])
    dts = (F32, F32, BF16, F32, BF16, F32, BF16, F32)
    return pl.pallas_call(
        functools.partial(_odd_in_kernel, nseq),
        grid=(T // tm,),
        in_specs=[pl.BlockSpec((tm, D), row)] + [_resident(w.shape) for w in ws] + [tab, tab]
                 + [_resident(t.shape) for t in consts],
        out_specs=[pl.BlockSpec((tm, n), row) for n in widths],
        out_shape=[jax.ShapeDtypeStruct((T, n), dt) for n, dt in zip(widths, dts)],
        scratch_shapes=[pltpu.VMEM((SUBLANES + tm, wx.shape[1]), F32)],
        compiler_params=_cparams("arbitrary"),
        name="odd_in",
    )(x2, *ws, cos_t, sin_t, *consts)


def _ssd_tile(z_ref, xs_ref, bc_ref, dt_ref, alog_ref, dskip_ref, ng_ref,
              expand_ref, tri_ref, o_ref, state_ref, y_ref):
    L = C_CHUNK
    nchunk = z_ref.shape[0] // L
    dt = dt_ref[...]
    dta = dt * (-jnp.exp(alog_ref[...]))
    acs_c = [_cumsum_rows(dta[c * L:(c + 1) * L], tri_ref[...]) for c in range(nchunk)]
    acs = jnp.concatenate(acs_c, axis=0)
    last = jnp.concatenate([jnp.broadcast_to(t[L - 1:L, :], t.shape) for t in acs_c], axis=0)
    to_state_e = _expand_heads(dt * jnp.exp(last - acs), expand_ref[...])
    from_start_e = _expand_heads(jnp.exp(acs), expand_ref[...])
    yield
    for c in range(nchunk):
        rows = slice(c * L, (c + 1) * L)
        yield from _ssd_chunk(z_ref.at[rows], xs_ref.at[rows], bc_ref.at[rows], acs_c[c], dt[rows],
                              to_state_e[rows], from_start_e[rows], dskip_ref, ng_ref, o_ref.at[rows],
                              state_ref, y_ref)


def _ssd_chunk(z_ref, xs_ref, bc_ref, acs, dt, to_state_e, from_start_e, dskip_ref, ng_ref,
               o_ref, state_ref, y_ref):
    L = C_CHUNK
    gw = C_STATE
    hpg = C_HEADS // C_GROUPS
    gcols = hpg * C_HEAD_DIM
    xs = xs_ref[...]
    xs_b = xs.astype(BF16)
    acs_t = acs.T
    dt_t = dt.T
    cdec_e = from_start_e[L - 1:L, :]
    wst = (xs * to_state_e).astype(BF16)

    row = lax.broadcasted_iota(jnp.int32, (L, L), 0)
    col = lax.broadcasted_iota(jnp.int32, (L, L), 1)
    causal = col <= row
    lane = lax.broadcasted_iota(jnp.int32, (L, LANES), 1)
    lo = lane < C_HEAD_DIM
    zero_b = jnp.zeros((L, LANES), BF16)
    quad = 2 * LANES

    for g in range(C_GROUPS):
        bm = bc_ref[:, g * gw:(g + 1) * gw]
        cm = bc_ref[:, (C_GROUPS + g) * gw:(C_GROUPS + g + 1) * gw]
        cbm = _dot_nt(cm, bm)
        gs = slice(g * gcols, (g + 1) * gcols)
        prev = state_ref[:, gs]
        y_off = _dot(cm, prev.astype(BF16)) * from_start_e[:, gs]
        new = _dot_tn(bm, wst[:, gs])
        state_ref[:, gs] = prev * cdec_e[:, gs] + new
        for qd in range(gcols // quad):
            ms, xr = [], []
            for jp in range(2):
                c0 = g * gcols + qd * quad + jp * LANES
                xp = xs_b[:, c0:c0 + LANES]
                for half in range(2):
                    h = (c0 // C_HEAD_DIM) + half
                    seg = acs[:, h:h + 1] - acs_t[h:h + 1, :]
                    dec = jnp.exp(jnp.where(causal, seg, -jnp.inf))
                    ms.append((cbm * dec * dt_t[h:h + 1, :]).astype(BF16))
                    xh = jnp.where(lo if half == 0 else jnp.logical_not(lo), xp, zero_b)
                    xr.append(jnp.concatenate([xh, zero_b] if jp == 0 else [zero_b, xh], axis=1))
            cols = slice(g * gcols + qd * quad, g * gcols + (qd + 1) * quad)
            y = (_dot(jnp.concatenate(ms, axis=1), jnp.concatenate(xr, axis=0))
                 + y_off[:, qd * quad:(qd + 1) * quad] + xs[:, cols] * dskip_ref[:, cols])
            y_ref[:, cols] = y * z_ref[:, cols]
            yield

    for g in range(C_GROUPS):
        gs = slice(g * gcols, (g + 1) * gcols)
        yg = y_ref[:, gs]
        ms = jnp.mean(yg * yg, axis=-1, keepdims=True)
        o_ref[:, gs] = (yg * lax.rsqrt(ms + NORM_EPS) * ng_ref[:, gs]).astype(BF16)
    yield


def _expand_heads(v, expand3):
    lane = lax.broadcasted_iota(jnp.int32, v.shape, 1)
    hi, mid, low = _split_bf16(jnp.where(lane < C_HEADS, v, 0.0), 3)
    packed = (hi.astype(F32) + pltpu.roll(mid.astype(F32), C_HEADS, 1)
              + pltpu.roll(low.astype(F32), 2 * C_HEADS, 1))
    return _dot(packed.astype(BF16), expand3)


def _cumsum_rows(v, tri):
    acc = None
    for p in _split_bf16(v, 3):
        t = _dot(tri, p)
        acc = t if acc is None else acc + t
    return acc


def _ret_tile(q_ref, k_ref, v_ref, g_ref, dmat_ref, toend_ref, fstart_ref, cdec_ref, gng_ref, gnb_ref,
              o_ref, state_ref):
    L = D_CHUNK
    for r0 in range(0, q_ref.shape[0], L):
        rows = slice(r0, r0 + L)
        for h in range(D_HEADS):
            ks = slice(h * D_QK_DIM, (h + 1) * D_QK_DIM)
            vs = slice(h * D_V_DIM, (h + 1) * D_V_DIM)
            qh = q_ref[rows, ks]
            kh = k_ref[rows, ks]
            vh = v_ref[rows, vs]
            scores = _dot_nt(qh, kh.astype(BF16)) * dmat_ref[h]
            prev = state_ref[:, vs]
            y = _dot(scores.astype(BF16), vh) + _dot(qh, prev.astype(BF16)) * fstart_ref[:, vs]
            kd = (kh * toend_ref[:, ks]).astype(BF16)
            state_ref[:, vs] = prev * cdec_ref[:, vs] + _dot_tn(kd, vh)
            mu = jnp.mean(y, axis=-1, keepdims=True)
            d = y - mu
            var = jnp.mean(d * d, axis=-1, keepdims=True)
            yn = d * lax.rsqrt(var + NORM_EPS) * gng_ref[:, vs] + gnb_ref[:, vs]
            o_ref[rows, vs] = (g_ref[rows, vs] * yn).astype(BF16)
            yield


N_SSD_TILED = 4
N_SSD_CONST = 5
N_RET_TILED = 4
N_RET_CONST = 6


def _scan_kernel(*refs):
    n_ssd = N_SSD_TILED + N_SSD_CONST
    n_in = n_ssd + N_RET_TILED + N_RET_CONST
    ssd_in, ret_in = refs[:n_ssd], refs[n_ssd:n_in]
    o_ssd, o_ret, ssd_state, y_ref, ret_state = refs[n_in:]

    @pl.when(pl.program_id(1) == 0)
    def _():
        ssd_state[...] = jnp.zeros_like(ssd_state)
        ret_state[...] = jnp.zeros_like(ret_state)

    pending = [_ssd_tile(*ssd_in, o_ssd, ssd_state, y_ref), _ret_tile(*ret_in, o_ret, ret_state)]
    while pending:
        for gen in list(pending):
            if next(gen, "done") == "done":
                pending.remove(gen)


def _scans(ssd_tiled, ssd_consts, ret_tiled, ret_consts, batch, seq):
    tr = min(SCAN_TILE, seq)
    nt = seq // tr
    cur = lambda b, c: (b * nt + c, 0)
    tiled = lambda ts: [pl.BlockSpec((tr, t.shape[1]), cur) for t in ts]
    const = lambda ts: [_resident(t.shape) for t in ts]
    z, v = ssd_tiled[0], ret_tiled[2]
    return pl.pallas_call(
        _scan_kernel,
        grid=(batch, nt),
        in_specs=tiled(ssd_tiled) + const(ssd_consts) + tiled(ret_tiled) + const(ret_consts),
        out_specs=[pl.BlockSpec((tr, z.shape[1]), cur), pl.BlockSpec((tr, v.shape[1]), cur)],
        out_shape=[jax.ShapeDtypeStruct(z.shape, BF16), jax.ShapeDtypeStruct(v.shape, BF16)],
        scratch_shapes=[pltpu.VMEM((C_STATE, z.shape[1]), F32),
                        pltpu.VMEM((C_CHUNK, z.shape[1]), F32),
                        pltpu.VMEM((D_QK_DIM, v.shape[1]), F32)],
        compiler_params=_cparams("arbitrary", "arbitrary"),
        name="ssd_retention_scan",
    )(*ssd_tiled, *ssd_consts, *ret_tiled, *ret_consts)


def _even_tables(seq):
    half = A_ROPE_DIM // 2
    inv = jnp.power(jnp.float32(ROPE_THETA), -jnp.arange(half, dtype=F32) / half)
    ang = jnp.arange(seq, dtype=F32)[:, None] * inv[None, :]
    cos, sin = jnp.cos(ang), jnp.sin(ang)
    ones = jnp.ones((seq, A_HEAD_DIM - 2 * half), F32)
    zeros = jnp.zeros((seq, A_HEAD_DIM - 2 * half), F32)
    zh = jnp.zeros((seq, half), F32)
    c = jnp.concatenate([cos, cos, ones], axis=1)
    sa = jnp.concatenate([zh, sin, zeros], axis=1)
    sb = jnp.concatenate([-sin, zh, zeros], axis=1)
    rep = LANES // A_HEAD_DIM
    return jnp.tile(c, (1, rep)), jnp.tile(sa, (1, rep)), jnp.tile(sb, (1, rep))


def _ret_tables(seq):
    half = D_QK_DIM // 2
    inv = 1.0 / jnp.power(jnp.float32(RET_THETA), jnp.linspace(0.0, 1.0, half, dtype=F32))
    ang = jnp.arange(seq, dtype=F32)[:, None] * inv[None, :]
    cos, sin = jnp.cos(ang), jnp.sin(ang)
    L = D_CHUNK
    log_gamma = jnp.log(1.0 - jnp.power(2.0, -5.0 - jnp.arange(D_HEADS, dtype=F32)))
    idx = jnp.arange(L, dtype=F32)
    rel = idx[:, None] - idx[None, :]
    dmat = jnp.where(rel[None] >= 0, jnp.exp(log_gamma[:, None, None] * jnp.maximum(rel, 0.0)[None]), 0.0)
    to_end = jnp.exp(log_gamma[None, :] * (L - 1 - idx)[:, None])
    from_start = jnp.exp(log_gamma[None, :] * (idx + 1.0)[:, None])
    chunk_decay = jnp.exp(log_gamma * L)
    return (jnp.concatenate([cos, cos], axis=1), jnp.concatenate([-sin, sin], axis=1), dmat,
            jnp.repeat(to_end, D_QK_DIM, axis=1), jnp.repeat(from_start, D_V_DIM, axis=1),
            jnp.repeat(chunk_decay, D_V_DIM)[None, :])


def _row(v):
    return v.astype(F32)[None, :]


def _sublane_rep(v):
    v = v.astype(F32)
    return jnp.broadcast_to(v[..., None, :], v.shape[:-1] + (SUBLANES, v.shape[-1]))


def _pad_lanes(v, fill=0.0):
    pad = (-v.shape[-1]) % LANES
    return jnp.pad(v, [(0, 0)] * (v.ndim - 1) + [(0, pad)], constant_values=fill)


def _even_mixer(x2, w_in, sinks, dw_w, dw_b, cn_g, cn_b, w_out, tables, batch, seq):
    nq = A_HEADS * A_HEAD_DIM
    nkv = A_KV_HEADS * A_HEAD_DIM
    wq = w_in[:, :nq].astype(BF16)
    wkv = w_in[:, nq:nq + 2 * nkv].astype(BF16)
    wu = w_in[:, nq + 2 * nkv:].astype(BF16)
    q, k, v, hglu = _even_in(x2, wq, wkv, wu, *tables, seq)
    y_attn = _attention(sinks.astype(F32), q, k, v, batch, seq)
    y_conv = _conv_module(hglu, _sublane_rep(dw_w), _sublane_rep(dw_b), _sublane_rep(cn_g), _sublane_rep(cn_b),
                          batch, seq)
    wo = w_out.astype(BF16)
    return y_attn, y_conv, wo[:nq], wo[nq:]


def _odd_mixer(x2, w_in, conv_w, conv_b, dt_bias, a_log, d_skip, ssm_norm_g, gn_g, gn_b, w_out,
               tables, consts, batch, seq):
    inner = C_HEADS * C_HEAD_DIM
    xbc_w = inner + 2 * C_GROUPS * C_STATE
    nqk = D_HEADS * D_QK_DIM
    nv = D_HEADS * D_V_DIM
    sizes = (inner, xbc_w, C_HEADS, nqk, nqk, nv, nv)
    offs = [0]
    for s in sizes:
        offs.append(offs[-1] + s)
    wb = w_in.astype(BF16)
    wz, wx, wdt, wq, wk, wv, wg = [wb[:, offs[i]:offs[i + 1]] for i in range(len(sizes))]
    wdt = _pad_lanes(wdt)
    cos_t, sin_t, dmat, toend, fstart, cdec = tables
    expand, tri = consts
    z, xs, bc, dt, rq, rk, rv, rg = _odd_in(x2, wz, wx, wdt, wq, wk, wv, wg, cos_t, sin_t,
                                            _sublane_rep(conv_w), _sublane_rep(conv_b),
                                            _pad_lanes(_row(dt_bias)), seq)
    ssd_consts = (_pad_lanes(_row(a_log)), _row(jnp.repeat(d_skip.astype(F32), C_HEAD_DIM)), _row(ssm_norm_g),
                  expand, tri)
    ret_consts = (dmat, toend, fstart, cdec, _row(gn_g), _row(gn_b))
    y_ssm, y_ret = _scans((z, xs, bc, dt), ssd_consts, (rq, rk, rv, rg), ret_consts, batch, seq)
    wo = w_out.astype(BF16)
    return y_ssm, y_ret, wo[:inner], wo[inner:]


def _ssd_consts():
    k = jnp.arange(LANES)[:, None]
    cols = jnp.arange(C_HEADS * C_HEAD_DIM)[None, :] // C_HEAD_DIM
    expand = jnp.logical_and(k % C_HEADS == cols, k < 3 * C_HEADS).astype(BF16)
    i = jnp.arange(C_CHUNK)
    tri = (i[None, :] <= i[:, None]).astype(BF16)
    return expand, tri


def kernel(x, ln_mix_g, ln_mix_b, ln_ffn_g, ln_ffn_b, ffn_w_gate, ffn_w_up, ffn_w_down, ev_w_in, ev_sinks, ev_dw_w, ev_dw_b, ev_cn_g, ev_cn_b, ev_w_out, od_w_in, od_conv_w, od_conv_b, od_dt_bias, od_a_log, od_d_skip, od_ssm_norm_g, od_ret_gn_g, od_ret_gn_b, od_w_out):
    batch, seq, d_model = x.shape
    depth = ln_mix_g.shape[0]
    alpha = float((2 * depth) ** 0.25)
    ev_tables = _even_tables(seq)
    od_tables = _ret_tables(seq)
    od_consts = _ssd_consts()
    ffn_w = (ffn_w_gate.astype(BF16), ffn_w_up.astype(BF16), ffn_w_down.astype(BF16))
    x2 = x.reshape(batch * seq, d_model)
    for layer in range(depth):
        i = layer // 2
        if layer % 2 == 0:
            mixed = _even_mixer(x2, ev_w_in[i], ev_sinks[i], ev_dw_w[i], ev_dw_b[i], ev_cn_g[i], ev_cn_b[i],
                                ev_w_out[i], ev_tables, batch, seq)
        else:
            mixed = _odd_mixer(x2, od_w_in[i], od_conv_w[i], od_conv_b[i], od_dt_bias[i], od_a_log[i],
                               od_d_skip[i], od_ssm_norm_g[i], od_ret_gn_g[i], od_ret_gn_b[i], od_w_out[i],
                               od_tables, od_consts, batch, seq)
        x2 = _mixer_out_ffn(mixed, x2, ln_mix_g[layer], ln_mix_b[layer], ffn_w, layer,
                            ln_ffn_g[layer], ln_ffn_b[layer], alpha)
    return x2.reshape(batch, seq, d_model)


def _mixer_out_ffn(mixed, x2, ln1_g, ln1_b, ffn_w, layer, ln2_g, ln2_b, alpha):
    ya, yb, wa, wb = mixed
    return _out_ffn(ya, yb, x2, wa, wb, _row(ln1_g), _row(ln1_b), *ffn_w, _row(ln2_g), _row(ln2_b), layer, alpha)
```

```python
import functools
import math

import jax
import jax.numpy as jnp
from jax import lax
from jax.experimental import pallas as pl
from jax.experimental.pallas import tpu as pltpu

F32 = jnp.float32
BF16 = jnp.bfloat16

A_HEADS = 8
A_KV_HEADS = 2
A_HEAD_DIM = 64
A_WINDOW = 128
ROPE_THETA = 500000.0
A_ROPE_DIM = A_HEAD_DIM // 4
B_CONV = 31
C_HEADS = 16
C_HEAD_DIM = 64
C_GROUPS = 2
C_STATE = 128
C_CONV = 4
C_CHUNK = 128
D_HEADS = 4
D_QK_DIM = 128
D_V_DIM = 256
D_CHUNK = 128
RET_THETA = 10000.0
NORM_EPS = 1e-5

LANES = 128
SUBLANES = 8
VMEM_LIMIT = 56 * 1024 * 1024

ROW_TILE = 512
ATTN_TILE = 512
CONV_ROWS = 32
CONV_HALO = 32
FFN_CHUNK = 256
PROJ_COLS = 256
CONV4_ROWS = 256
SCAN_TILE = 512


def _cparams(*sem):
    return pltpu.CompilerParams(dimension_semantics=sem, vmem_limit_bytes=VMEM_LIMIT)


def _resident(shape):
    nd = len(shape)
    return pl.BlockSpec(shape, lambda *_: (0,) * nd, pipeline_mode=pl.Buffered(1))


def _resident_layer(shape, layer):
    return pl.BlockSpec((None,) + tuple(shape[1:]), lambda *_: (layer, 0, 0), pipeline_mode=pl.Buffered(1))


def _layer_norm(y, g, b):
    mu = jnp.mean(y, axis=-1, keepdims=True)
    d = y - mu
    var = jnp.mean(d * d, axis=-1, keepdims=True)
    return d * lax.rsqrt(var + NORM_EPS) * g + b


def _silu(v):
    h = 0.5 * v
    return h + h * jnp.tanh(h)


def _dot(a, b):
    return jnp.dot(a, b, preferred_element_type=F32)


def _dot_nt(a, b):
    return lax.dot_general(a, b, (((1,), (1,)), ((), ())), preferred_element_type=F32)


def _dot_tn(a, b):
    return lax.dot_general(a, b, (((0,), (0,)), ((), ())), preferred_element_type=F32)


def _split_bf16(v, parts):
    out = []
    r = v
    for _ in range(parts):
        p = r.astype(BF16)
        out.append(p)
        r = r - p.astype(F32)
    return out


def _even_in_kernel(nseq, x_ref, wq_ref, wkv_ref, wu_ref, cos_ref, sa_ref, sb_ref,
                    cw_ref, cb_ref, cg_ref, cbeta_ref,
                    q_ref, k_ref, v_ref, yc_ref, var_ref, halo_ref):
    i = pl.program_id(0)
    tm = x_ref.shape[0]
    xb = x_ref[...].astype(BF16)
    c, sa, sb = cos_ref[...], sa_ref[...], sb_ref[...]
    rot_dim = A_ROPE_DIM // 2

    def rot(t):
        return t * c + pltpu.roll(t, rot_dim, 1) * sa + pltpu.roll(t, LANES - rot_dim, 1) * sb

    @pl.when(i % nseq == 0)
    def _():
        halo_ref[...] = jnp.zeros_like(halo_ref)

    u = _dot(xb, wu_ref[...])
    C = u.shape[1] // 2
    ext_rows = CONV_HALO + tm
    for c0 in range(0, C, LANES):
        cols = slice(c0, c0 + LANES)
        hglu = u[:, c0:c0 + LANES] * jax.nn.sigmoid(u[:, C + c0:C + c0 + LANES])
        ext = jnp.concatenate([halo_ref[:, cols], hglu], axis=0)
        halo_ref[:, cols] = hglu[tm - CONV_HALO:, :]
        var_ref[0, :, :, cols] = ext.reshape(ext_rows // SUBLANES, SUBLANES, LANES)
        for r in range(1, SUBLANES):
            sh = pltpu.roll(ext, ext_rows - r, 0)
            var_ref[r, :, :, cols] = sh.reshape(ext_rows // SUBLANES, SUBLANES, LANES)

    def conv_pieces():
        first = CONV_HALO - (B_CONV - 1)
        nt = CONV_ROWS // SUBLANES
        for r0 in range(0, tm, CONV_ROWS):
            acc = None
            for kk in range(B_CONV):
                d = first + kk
                s = r0 // SUBLANES + d // SUBLANES
                t = cw_ref[kk] * var_ref[d % SUBLANES, s:s + nt]
                acc = t + cb_ref[...] if acc is None else acc + t
            y = _layer_norm(acc, cg_ref[...], cbeta_ref[...])
            yc_ref[r0:r0 + CONV_ROWS, :] = _silu(y).reshape(CONV_ROWS, C).astype(BF16)
            yield

    def proj_pieces():
        qscale = A_HEAD_DIM ** -0.5
        for c0 in range(0, wq_ref.shape[1], PROJ_COLS):
            q = _dot(xb, wq_ref[:, c0:c0 + PROJ_COLS])
            for j in range(PROJ_COLS // LANES):
                q_ref[:, c0 + j * LANES:c0 + (j + 1) * LANES] = (
                    rot(q[:, j * LANES:(j + 1) * LANES]) * qscale).astype(BF16)
            yield
        kv = _dot(xb, wkv_ref[...])
        k2 = rot(kv[:, :LANES])
        v2 = kv[:, LANES:]
        k2s = pltpu.roll(k2, A_HEAD_DIM, 1)
        v2s = pltpu.roll(v2, A_HEAD_DIM, 1)
        lo = lax.broadcasted_iota(jnp.int32, k2.shape, 1) < A_HEAD_DIM
        zero = jnp.zeros_like(v2)
        k_ref[:, :LANES] = jnp.where(lo, k2, k2s).astype(BF16)
        k_ref[:, LANES:] = jnp.where(lo, k2s, k2).astype(BF16)
        for j, blk in enumerate((jnp.where(lo, v2, zero), jnp.where(lo, zero, v2s),
                                 jnp.where(lo, v2s, zero), jnp.where(lo, zero, v2))):
            v_ref[:, j * LANES:(j + 1) * LANES] = blk.astype(BF16)
        yield

    conv, proj = conv_pieces(), proj_pieces()
    per = max(1, (tm // CONV_ROWS) // 4)
    while True:
        done = [next(conv, "done") for _ in range(per)][-1] == "done"
        if next(proj, "done") == "done" and done:
            break


def _even_in(x2, wq, wkv, wu, cos_t, sa_t, sb_t, cw, cb, cg, cbeta, seq):
    T, D = x2.shape
    assert wkv.shape[1] == 2 * LANES, "k and v must fill exactly 128 lanes each"
    tm = min(ROW_TILE, seq)
    nseq = seq // tm
    C = wu.shape[1] // 2
    row = lambda i: (i, 0)
    tab = pl.BlockSpec((tm, LANES), lambda i: (i % nseq, 0))
    consts = (cw, cb, cg, cbeta)
    widths = (wq.shape[1], 2 * LANES, 4 * LANES, C)
    return pl.pallas_call(
        functools.partial(_even_in_kernel, nseq),
        grid=(T // tm,),
        in_specs=[pl.BlockSpec((tm, D), row), _resident(wq.shape), _resident(wkv.shape), _resident(wu.shape),
                  tab, tab, tab] + [_resident(t.shape) for t in consts],
        out_specs=[pl.BlockSpec((tm, n), row) for n in widths],
        out_shape=[jax.ShapeDtypeStruct((T, n), BF16) for n in widths],
        scratch_shapes=[pltpu.VMEM((SUBLANES, (CONV_HALO + tm) // SUBLANES, SUBLANES, C), F32),
                        pltpu.VMEM((CONV_HALO, C), F32)],
        compiler_params=_cparams("arbitrary"),
        name="even_in_conv",
    )(x2, wq, wkv, wu, cos_t, sa_t, sb_t, *consts)


def _attn_kernel(sink_ref, q_ref, kp_ref, kc_ref, vp_ref, vc_ref, o_ref):
    n = pl.program_id(1)
    W = A_WINDOW
    G = A_HEADS // A_KV_HEADS
    nsub = q_ref.shape[0] // W
    r = lax.broadcasted_iota(jnp.int32, (G * W, 2 * W), 0)
    i = jnp.bitwise_and(r, W - 1)
    c = lax.broadcasted_iota(jnp.int32, (G * W, 2 * W), 1)
    band = jnp.logical_and(c > i, c <= i + W)
    neg = jnp.float32(-jnp.inf)
    bias_inner = jnp.where(band, 0.0, neg)
    bias_first = jnp.where(jnp.logical_and(band, jnp.logical_or(c >= W, n > 0)), 0.0, neg)
    rg = lax.broadcasted_iota(jnp.int32, (G * W, 1), 0) // W
    lane = lax.broadcasted_iota(jnp.int32, (W, LANES), 1)
    lo = lane < A_HEAD_DIM
    ones = jnp.ones((2 * W, LANES), BF16)
    for sub in range(nsub):
        rows = slice(sub * W, (sub + 1) * W)
        prow = slice((sub - 1) * W, sub * W)
        bias = bias_first if sub == 0 else bias_inner
        for h in range(A_KV_HEADS):
            kl = slice(h * LANES, (h + 1) * LANES)
            kprev = kp_ref[:, kl] if sub == 0 else kc_ref[prow, kl]
            kcat = jnp.concatenate([kprev, kc_ref[rows, kl]], axis=0)
            parts = []
            for jj in range(G // 2):
                j = h * (G // 2) + jj
                q2 = q_ref[rows, j * LANES:(j + 1) * LANES]
                parts.append(jnp.where(lo, q2, jnp.zeros_like(q2)))
                parts.append(jnp.where(lo, jnp.zeros_like(q2), q2))
            s = _dot_nt(jnp.concatenate(parts, axis=0), kcat) + bias
            sink = jnp.full((G * W, 1), sink_ref[h * G], F32)
            for gi in range(1, G):
                sink = jnp.where(rg == gi, sink_ref[h * G + gi], sink)
            m = jnp.maximum(jnp.max(jnp.maximum(s[:, :W], s[:, W:]), axis=-1, keepdims=True), sink)
            p = jnp.exp(s - m).astype(BF16)
            esink = jnp.exp(sink - m)
            vaug = []
            for half in range(2):
                vl = slice((2 * h + half) * LANES, (2 * h + half + 1) * LANES)
                vprev = vp_ref[:, vl] if sub == 0 else vc_ref[prow, vl]
                vcat = jnp.concatenate([vprev, vc_ref[rows, vl]], axis=0)
                vaug.append(jnp.concatenate([vcat, ones], axis=1))
            for jj in range(G // 2):
                j = h * (G // 2) + jj
                he = slice((2 * jj) * W, (2 * jj + 1) * W)
                ho = slice((2 * jj + 1) * W, (2 * jj + 2) * W)
                re = _dot(p[he], vaug[0])
                ro = _dot(p[ho], vaug[1])
                acc = re[:, :LANES] + ro[:, :LANES]
                d = jnp.where(lo, re[:, LANES:] + esink[he], ro[:, LANES:] + esink[ho])
                o_ref[rows, j * LANES:(j + 1) * LANES] = (acc * (1.0 / d)).astype(BF16)


def _attention(sinks, q, k, v, batch, seq):
    W = A_WINDOW
    tq = min(ATTN_TILE, seq)
    per = tq // W
    nt = seq // tq
    cur = lambda b, n: (b * nt + n, 0)
    prev = lambda b, n: (jnp.maximum((b * nt + n) * per - 1, 0), 0)
    return pl.pallas_call(
        _attn_kernel,
        grid=(batch, nt),
        in_specs=[pl.BlockSpec(memory_space=pltpu.SMEM),
                  pl.BlockSpec((tq, q.shape[1]), cur),
                  pl.BlockSpec((W, k.shape[1]), prev), pl.BlockSpec((tq, k.shape[1]), cur),
                  pl.BlockSpec((W, v.shape[1]), prev), pl.BlockSpec((tq, v.shape[1]), cur)],
        out_specs=pl.BlockSpec((tq, q.shape[1]), cur),
        out_shape=jax.ShapeDtypeStruct(q.shape, BF16),
        compiler_params=_cparams("parallel", "parallel"),
        name="swa_attention",
    )(sinks, q, k, k, v, v)


def _out_ffn_kernel(alpha, ya_ref, yb_ref, x_ref, wa_ref, wb_ref, g1_ref, b1_ref,
                    wg_ref, wu_ref, wd_ref, g2_ref, b2_ref, o_ref, x1_ref, xb_ref):
    mix = _dot(ya_ref[...], wa_ref[...]) + _dot(yb_ref[...], wb_ref[...])
    x1 = _layer_norm(alpha * x_ref[...] + mix, g1_ref[...], b1_ref[...])
    x1_ref[...] = x1
    xb_ref[...] = x1.astype(BF16)
    hidden = wg_ref.shape[1]
    acc = None
    for c0 in range(0, hidden, FFN_CHUNK):
        c1 = min(c0 + FFN_CHUNK, hidden)
        gate = _dot(xb_ref[...], wg_ref[:, c0:c1])
        up = _dot(xb_ref[...], wu_ref[:, c0:c1])
        t = _dot((_silu(gate) * up).astype(BF16), wd_ref[c0:c1, :])
        acc = t if acc is None else acc + t
    o_ref[...] = _layer_norm(alpha * x1_ref[...] + acc, g2_ref[...], b2_ref[...])


def _out_ffn(ya, yb, x2, wa, wb, g1, b1, wg, wu, wd, g2, b2, layer, alpha):
    T, D = x2.shape
    tm = min(ROW_TILE, T)
    row = lambda i: (i, 0)
    consts = (wa, wb, g1, b1, wg, wu, wd, g2, b2)
    const_specs = [_resident_layer(t.shape, layer) if t.ndim == 3 else _resident(t.shape) for t in consts]
    return pl.pallas_call(
        functools.partial(_out_ffn_kernel, alpha),
        grid=(T // tm,),
        in_specs=[pl.BlockSpec((tm, ya.shape[1]), row), pl.BlockSpec((tm, yb.shape[1]), row),
                  pl.BlockSpec((tm, D), row)] + const_specs,
        out_specs=pl.BlockSpec((tm, D), row),
        out_shape=jax.ShapeDtypeStruct((T, D), F32),
        scratch_shapes=[pltpu.VMEM((tm, D), F32), pltpu.VMEM((tm, D), BF16)],
        compiler_params=_cparams("parallel"),
        name="out_ffn",
    )(ya, yb, x2, *consts)


def _odd_in_kernel(nseq, x_ref, wz_ref, wx_ref, wdt_ref, wq_ref, wk_ref, wv_ref, wg_ref, cos_ref, sin_ref,
                   cw_ref, cb_ref, dtb_ref,
                   z_ref, xs_ref, bc_ref, dt_ref, q_ref, k_ref, v_ref, g_ref, buf_ref):
    i = pl.program_id(0)
    tm = x_ref.shape[0]
    inner = xs_ref.shape[1]
    xb = x_ref[...].astype(BF16)

    def col_blocks(w_ref):
        for c0 in range(0, w_ref.shape[1], PROJ_COLS):
            c1 = min(c0 + PROJ_COLS, w_ref.shape[1])
            yield c0, c1, _dot(xb, w_ref[:, c0:c1])

    @pl.when(i % nseq == 0)
    def _():
        buf_ref[0:SUBLANES, :] = jnp.zeros((SUBLANES, buf_ref.shape[1]), F32)

    @pl.when(i % nseq != 0)
    def _():
        buf_ref[0:SUBLANES, :] = buf_ref[tm:tm + SUBLANES, :]

    rblk = min(CONV4_ROWS, tm)
    ntile = (SUBLANES + rblk) // SUBLANES

    def conv_block(r0, c0):
        cols = slice(c0, c0 + LANES)
        ext = buf_ref[r0:r0 + SUBLANES + rblk, cols]
        acc = cb_ref[:, cols] + cw_ref[C_CONV - 1, :, cols] * ext.reshape(ntile, SUBLANES, LANES)
        for kk in range(1, C_CONV):
            sh = pltpu.roll(ext, kk, 0).reshape(ntile, SUBLANES, LANES)
            acc = acc + cw_ref[C_CONV - 1 - kk, :, cols] * sh
        y = _silu(acc[1:]).reshape(rblk, LANES)
        if c0 < inner:
            xs_ref[r0:r0 + rblk, cols] = y
        else:
            bc_ref[r0:r0 + rblk, c0 - inner:c0 - inner + LANES] = y.astype(BF16)

    for c0, c1, r in col_blocks(wx_ref):
        buf_ref[SUBLANES:SUBLANES + tm, c0:c1] = r
        for cc in range(c0, c1, LANES):
            for r0 in range(0, tm, rblk):
                conv_block(r0, cc)

    for c0, c1, r in col_blocks(wz_ref):
        z_ref[:, c0:c1] = _silu(r)

    c, s = cos_ref[...], sin_ref[...]
    kscale = D_QK_DIM ** -0.5

    def rot(t):
        return t * c + pltpu.roll(t, D_QK_DIM // 2, 1) * s

    for c0, c1, r in col_blocks(wq_ref):
        for cc in range(c0, c1, D_QK_DIM):
            q_ref[:, cc:cc + D_QK_DIM] = rot(r[:, cc - c0:cc - c0 + D_QK_DIM]).astype(BF16)
    for c0, c1, r in col_blocks(wk_ref):
        for cc in range(c0, c1, D_QK_DIM):
            k_ref[:, cc:cc + D_QK_DIM] = rot(r[:, cc - c0:cc - c0 + D_QK_DIM]) * kscale
    for c0, c1, r in col_blocks(wv_ref):
        v_ref[:, c0:c1] = r.astype(BF16)
    for c0, c1, r in col_blocks(wg_ref):
        g_ref[:, c0:c1] = _silu(r)
    dtv = _dot(xb, wdt_ref[...]) + dtb_ref[...]
    dt_ref[...] = jnp.maximum(dtv, 0.0) + jnp.log1p(jnp.exp(-jnp.abs(dtv)))


def _odd_in(x2, wz, wx, wdt, wq, wk, wv, wg, cos_t, sin_t, cw, cb, dtb, seq):
    T, D = x2.shape
    tm = min(ROW_TILE, seq)
    nseq = seq // tm
    inner = wz.shape[1]
    row = lambda i: (i, 0)
    tab = pl.BlockSpec((tm, LANES), lambda i: (i % nseq, 0))
    ws = (wz, wx, wdt, wq, wk, wv, wg)
    consts = (cw, cb, dtb)
    widths = (inner, inner, wx.shape[1] - inner, wdt.shape[1], wq.shape[1], wk.shape[1], wv.shape[1], wg.shape[1])
    dts = (F32, F32, BF16, F32, BF16, F32, BF16, F32)
    return pl.pallas_call(
        functools.partial(_odd_in_kernel, nseq),
        grid=(T // tm,),
        in_specs=[pl.BlockSpec((tm, D), row)] + [_resident(w.shape) for w in ws] + [tab, tab]
                 + [_resident(t.shape) for t in consts],
        out_specs=[pl.BlockSpec((tm, n), row) for n in widths],
        out_shape=[jax.ShapeDtypeStruct((T, n), dt) for n, dt in zip(widths, dts)],
        scratch_shapes=[pltpu.VMEM((SUBLANES + tm, wx.shape[1]), F32)],
        compiler_params=_cparams("arbitrary"),
        name="odd_in",
    )(x2, *ws, cos_t, sin_t, *consts)


def _ssd_tile(z_ref, xs_ref, bc_ref, dt_ref, alog_ref, dskip_ref, ng_ref,
              expand_ref, tri_ref, o_ref, state_ref, y_ref):
    L = C_CHUNK
    nchunk = z_ref.shape[0] // L
    dt = dt_ref[...]
    dta = dt * (-jnp.exp(alog_ref[...]))
    acs_c = [_cumsum_rows(dta[c * L:(c + 1) * L], tri_ref[...]) for c in range(nchunk)]
    acs = jnp.concatenate(acs_c, axis=0)
    last = jnp.concatenate([jnp.broadcast_to(t[L - 1:L, :], t.shape) for t in acs_c], axis=0)
    to_state_e = _expand_heads(dt * jnp.exp(last - acs), expand_ref[...])
    from_start_e = _expand_heads(jnp.exp(acs), expand_ref[...])
    yield
    for c in range(nchunk):
        rows = slice(c * L, (c + 1) * L)
        yield from _ssd_chunk(z_ref.at[rows], xs_ref.at[rows], bc_ref.at[rows], acs_c[c], dt[rows],
                              to_state_e[rows], from_start_e[rows], dskip_ref, ng_ref, o_ref.at[rows],
                              state_ref, y_ref)


def _ssd_chunk(z_ref, xs_ref, bc_ref, acs, dt, to_state_e, from_start_e, dskip_ref, ng_ref,
               o_ref, state_ref, y_ref):
    L = C_CHUNK
    gw = C_STATE
    hpg = C_HEADS // C_GROUPS
    gcols = hpg * C_HEAD_DIM
    xs = xs_ref[...]
    xs_b = xs.astype(BF16)
    acs_t = acs.T
    dt_t = dt.T
    cdec_e = from_start_e[L - 1:L, :]
    wst = (xs * to_state_e).astype(BF16)

    row = lax.broadcasted_iota(jnp.int32, (L, L), 0)
    col = lax.broadcasted_iota(jnp.int32, (L, L), 1)
    causal = col <= row
    lane = lax.broadcasted_iota(jnp.int32, (L, LANES), 1)
    lo = lane < C_HEAD_DIM
    zero_b = jnp.zeros((L, LANES), BF16)
    quad = 2 * LANES

    for g in range(C_GROUPS):
        bm = bc_ref[:, g * gw:(g + 1) * gw]
        cm = bc_ref[:, (C_GROUPS + g) * gw:(C_GROUPS + g + 1) * gw]
        cbm = _dot_nt(cm, bm)
        gs = slice(g * gcols, (g + 1) * gcols)
        prev = state_ref[:, gs]
        y_off = _dot(cm, prev.astype(BF16)) * from_start_e[:, gs]
        new = _dot_tn(bm, wst[:, gs])
        state_ref[:, gs] = prev * cdec_e[:, gs] + new
        for qd in range(gcols // quad):
            ms, xr = [], []
            for jp in range(2):
                c0 = g * gcols + qd * quad + jp * LANES
                xp = xs_b[:, c0:c0 + LANES]
                for half in range(2):
                    h = (c0 // C_HEAD_DIM) + half
                    seg = acs[:, h:h + 1] - acs_t[h:h + 1, :]
                    dec = jnp.exp(jnp.where(causal, seg, -jnp.inf))
                    ms.append((cbm * dec * dt_t[h:h + 1, :]).astype(BF16))
                    xh = jnp.where(lo if half == 0 else jnp.logical_not(lo), xp, zero_b)
                    xr.append(jnp.concatenate([xh, zero_b] if jp == 0 else [zero_b, xh], axis=1))
            cols = slice(g * gcols + qd * quad, g * gcols + (qd + 1) * quad)
            y = (_dot(jnp.concatenate(ms, axis=1), jnp.concatenate(xr, axis=0))
                 + y_off[:, qd * quad:(qd + 1) * quad] + xs[:, cols] * dskip_ref[:, cols])
            y_ref[:, cols] = y * z_ref[:, cols]
            yield

    for g in range(C_GROUPS):
        gs = slice(g * gcols, (g + 1) * gcols)
        yg = y_ref[:, gs]
        ms = jnp.mean(yg * yg, axis=-1, keepdims=True)
        o_ref[:, gs] = (yg * lax.rsqrt(ms + NORM_EPS) * ng_ref[:, gs]).astype(BF16)
    yield


def _expand_heads(v, expand3):
    lane = lax.broadcasted_iota(jnp.int32, v.shape, 1)
    hi, mid, low = _split_bf16(jnp.where(lane < C_HEADS, v, 0.0), 3)
    packed = (hi.astype(F32) + pltpu.roll(mid.astype(F32), C_HEADS, 1)
              + pltpu.roll(low.astype(F32), 2 * C_HEADS, 1))
    return _dot(packed.astype(BF16), expand3)


def _cumsum_rows(v, tri):
    acc = None
    for p in _split_bf16(v, 3):
        t = _dot(tri, p)
        acc = t if acc is None else acc + t
    return acc


def _ret_tile(q_ref, k_ref, v_ref, g_ref, dmat_ref, toend_ref, fstart_ref, cdec_ref, gng_ref, gnb_ref,
              o_ref, state_ref):
    L = D_CHUNK
    for r0 in range(0, q_ref.shape[0], L):
        rows = slice(r0, r0 + L)
        for h in range(D_HEADS):
            ks = slice(h * D_QK_DIM, (h + 1) * D_QK_DIM)
            vs = slice(h * D_V_DIM, (h + 1) * D_V_DIM)
            qh = q_ref[rows, ks]
            kh = k_ref[rows, ks]
            vh = v_ref[rows, vs]
            scores = _dot_nt(qh, kh.astype(BF16)) * dmat_ref[h]
            prev = state_ref[:, vs]
            y = _dot(scores.astype(BF16), vh) + _dot(qh, prev.astype(BF16)) * fstart_ref[:, vs]
            kd = (kh * toend_ref[:, ks]).astype(BF16)
            state_ref[:, vs] = prev * cdec_ref[:, vs] + _dot_tn(kd, vh)
            mu = jnp.mean(y, axis=-1, keepdims=True)
            d = y - mu
            var = jnp.mean(d * d, axis=-1, keepdims=True)
            yn = d * lax.rsqrt(var + NORM_EPS) * gng_ref[:, vs] + gnb_ref[:, vs]
            o_ref[rows, vs] = (g_ref[rows, vs] * yn).astype(BF16)
            yield


N_SSD_TILED = 4
N_SSD_CONST = 5
N_RET_TILED = 4
N_RET_CONST = 6


def _scan_kernel(*refs):
    n_ssd = N_SSD_TILED + N_SSD_CONST
    n_in = n_ssd + N_RET_TILED + N_RET_CONST
    ssd_in, ret_in = refs[:n_ssd], refs[n_ssd:n_in]
    o_ssd, o_ret, ssd_state, y_ref, ret_state = refs[n_in:]

    @pl.when(pl.program_id(1) == 0)
    def _():
        ssd_state[...] = jnp.zeros_like(ssd_state)
        ret_state[...] = jnp.zeros_like(ret_state)

    pending = [_ssd_tile(*ssd_in, o_ssd, ssd_state, y_ref), _ret_tile(*ret_in, o_ret, ret_state)]
    while pending:
        for gen in list(pending):
            if next(gen, "done") == "done":
                pending.remove(gen)


def _scans(ssd_tiled, ssd_consts, ret_tiled, ret_consts, batch, seq):
    tr = min(SCAN_TILE, seq)
    nt = seq // tr
    cur = lambda b, c: (b * nt + c, 0)
    tiled = lambda ts: [pl.BlockSpec((tr, t.shape[1]), cur) for t in ts]
    const = lambda ts: [_resident(t.shape) for t in ts]
    z, v = ssd_tiled[0], ret_tiled[2]
    return pl.pallas_call(
        _scan_kernel,
        grid=(batch, nt),
        in_specs=tiled(ssd_tiled) + const(ssd_consts) + tiled(ret_tiled) + const(ret_consts),
        out_specs=[pl.BlockSpec((tr, z.shape[1]), cur), pl.BlockSpec((tr, v.shape[1]), cur)],
        out_shape=[jax.ShapeDtypeStruct(z.shape, BF16), jax.ShapeDtypeStruct(v.shape, BF16)],
        scratch_shapes=[pltpu.VMEM((C_STATE, z.shape[1]), F32),
                        pltpu.VMEM((C_CHUNK, z.shape[1]), F32),
                        pltpu.VMEM((D_QK_DIM, v.shape[1]), F32)],
        compiler_params=_cparams("arbitrary", "arbitrary"),
        name="ssd_retention_scan",
    )(*ssd_tiled, *ssd_consts, *ret_tiled, *ret_consts)


def _even_tables(seq):
    half = A_ROPE_DIM // 2
    inv = jnp.power(jnp.float32(ROPE_THETA), -jnp.arange(half, dtype=F32) / half)
    ang = jnp.arange(seq, dtype=F32)[:, None] * inv[None, :]
    cos, sin = jnp.cos(ang), jnp.sin(ang)
    ones = jnp.ones((seq, A_HEAD_DIM - 2 * half), F32)
    zeros = jnp.zeros((seq, A_HEAD_DIM - 2 * half), F32)
    zh = jnp.zeros((seq, half), F32)
    c = jnp.concatenate([cos, cos, ones], axis=1)
    sa = jnp.concatenate([zh, sin, zeros], axis=1)
    sb = jnp.concatenate([-sin, zh, zeros], axis=1)
    rep = LANES // A_HEAD_DIM
    return jnp.tile(c, (1, rep)), jnp.tile(sa, (1, rep)), jnp.tile(sb, (1, rep))


def _ret_tables(seq):
    half = D_QK_DIM // 2
    inv = 1.0 / jnp.power(jnp.float32(RET_THETA), jnp.linspace(0.0, 1.0, half, dtype=F32))
    ang = jnp.arange(seq, dtype=F32)[:, None] * inv[None, :]
    cos, sin = jnp.cos(ang), jnp.sin(ang)
    L = D_CHUNK
    log_gamma = jnp.log(1.0 - jnp.power(2.0, -5.0 - jnp.arange(D_HEADS, dtype=F32)))
    idx = jnp.arange(L, dtype=F32)
    rel = idx[:, None] - idx[None, :]
    dmat = jnp.where(rel[None] >= 0, jnp.exp(log_gamma[:, None, None] * jnp.maximum(rel, 0.0)[None]), 0.0)
    to_end = jnp.exp(log_gamma[None, :] * (L - 1 - idx)[:, None])
    from_start = jnp.exp(log_gamma[None, :] * (idx + 1.0)[:, None])
    chunk_decay = jnp.exp(log_gamma * L)
    return (jnp.concatenate([cos, cos], axis=1), jnp.concatenate([-sin, sin], axis=1), dmat,
            jnp.repeat(to_end, D_QK_DIM, axis=1), jnp.repeat(from_start, D_V_DIM, axis=1),
            jnp.repeat(chunk_decay, D_V_DIM)[None, :])


def _row(v):
    return v.astype(F32)[None, :]


def _sublane_rep(v):
    v = v.astype(F32)
    return jnp.broadcast_to(v[..., None, :], v.shape[:-1] + (SUBLANES, v.shape[-1]))


def _pad_lanes(v, fill=0.0):
    pad = (-v.shape[-1]) % LANES
    return jnp.pad(v, [(0, 0)] * (v.ndim - 1) + [(0, pad)], constant_values=fill)


def _even_mixer(x2, w_in, sinks, dw_w, dw_b, cn_g, cn_b, w_out, tables, batch, seq):
    nq = A_HEADS * A_HEAD_DIM
    nkv = A_KV_HEADS * A_HEAD_DIM
    wq = w_in[:, :nq].astype(BF16)
    wkv = w_in[:, nq:nq + 2 * nkv].astype(BF16)
    wu = w_in[:, nq + 2 * nkv:].astype(BF16)
    q, k, v, y_conv = _even_in(x2, wq, wkv, wu, *tables, _sublane_rep(dw_w), _sublane_rep(dw_b),
                               _sublane_rep(cn_g), _sublane_rep(cn_b), seq)
    y_attn = _attention(sinks.astype(F32), q, k, v, batch, seq)
    wo = w_out.astype(BF16)
    return y_attn, y_conv, wo[:nq], wo[nq:]


def _odd_mixer(x2, w_in, conv_w, conv_b, dt_bias, a_log, d_skip, ssm_norm_g, gn_g, gn_b, w_out,
               tables, consts, batch, seq):
    inner = C_HEADS * C_HEAD_DIM
    xbc_w = inner + 2 * C_GROUPS * C_STATE
    nqk = D_HEADS * D_QK_DIM
    nv = D_HEADS * D_V_DIM
    sizes = (inner, xbc_w, C_HEADS, nqk, nqk, nv, nv)
    offs = [0]
    for s in sizes:
        offs.append(offs[-1] + s)
    wb = w_in.astype(BF16)
    wz, wx, wdt, wq, wk, wv, wg = [wb[:, offs[i]:offs[i + 1]] for i in range(len(sizes))]
    wdt = _pad_lanes(wdt)
    cos_t, sin_t, dmat, toend, fstart, cdec = tables
    expand, tri = consts
    z, xs, bc, dt, rq, rk, rv, rg = _odd_in(x2, wz, wx, wdt, wq, wk, wv, wg, cos_t, sin_t,
                                            _sublane_rep(conv_w), _sublane_rep(conv_b),
                                            _pad_lanes(_row(dt_bias)), seq)
    ssd_consts = (_pad_lanes(_row(a_log)), _row(jnp.repeat(d_skip.astype(F32), C_HEAD_DIM)), _row(ssm_norm_g),
                  expand, tri)
    ret_consts = (dmat, toend, fstart, cdec, _row(gn_g), _row(gn_b))
    y_ssm, y_ret = _scans((z, xs, bc, dt), ssd_consts, (rq, rk, rv, rg), ret_consts, batch, seq)
    wo = w_out.astype(BF16)
    return y_ssm, y_ret, wo[:inner], wo[inner:]


def _ssd_consts():
    k = jnp.arange(LANES)[:, None]
    cols = jnp.arange(C_HEADS * C_HEAD_DIM)[None, :] // C_HEAD_DIM
    expand = jnp.logical_and(k % C_HEADS == cols, k < 3 * C_HEADS).astype(BF16)
    i = jnp.arange(C_CHUNK)
    tri = (i[None, :] <= i[:, None]).astype(BF16)
    return expand, tri


def kernel(x, ln_mix_g, ln_mix_b, ln_ffn_g, ln_ffn_b, ffn_w_gate, ffn_w_up, ffn_w_down, ev_w_in, ev_sinks, ev_dw_w, ev_dw_b, ev_cn_g, ev_cn_b, ev_w_out, od_w_in, od_conv_w, od_conv_b, od_dt_bias, od_a_log, od_d_skip, od_ssm_norm_g, od_ret_gn_g, od_ret_gn_b, od_w_out):
    batch, seq, d_model = x.shape
    depth = ln_mix_g.shape[0]
    alpha = float((2 * depth) ** 0.25)
    ev_tables = _even_tables(seq)
    od_tables = _ret_tables(seq)
    od_consts = _ssd_consts()
    ffn_w = (ffn_w_gate.astype(BF16), ffn_w_up.astype(BF16), ffn_w_down.astype(BF16))
    x2 = x.reshape(batch * seq, d_model)
    for layer in range(depth):
        i = layer // 2
        if layer % 2 == 0:
            mixed = _even_mixer(x2, ev_w_in[i], ev_sinks[i], ev_dw_w[i], ev_dw_b[i], ev_cn_g[i], ev_cn_b[i],
                                ev_w_out[i], ev_tables, batch, seq)
        else:
            mixed = _odd_mixer(x2, od_w_in[i], od_conv_w[i], od_conv_b[i], od_dt_bias[i], od_a_log[i],
                               od_d_skip[i], od_ssm_norm_g[i], od_ret_gn_g[i], od_ret_gn_b[i], od_w_out[i],
                               od_tables, od_consts, batch, seq)
        x2 = _mixer_out_ffn(mixed, x2, ln_mix_g[layer], ln_mix_b[layer], ffn_w, layer,
                            ln_ffn_g[layer], ln_ffn_b[layer], alpha)
    return x2.reshape(batch, seq, d_model)


def _mixer_out_ffn(mixed, x2, ln1_g, ln1_b, ffn_w, layer, ln2_g, ln2_b, alpha):
    ya, yb, wa, wb = mixed
    return _out_ffn(ya, yb, x2, wa, wb, _row(ln1_g), _row(ln1_b), *ffn_w, _row(ln2_g), _row(ln2_b), layer, alpha)
```

```python
import functools
import math

import jax
import jax.numpy as jnp
from jax import lax
from jax.experimental import pallas as pl
from jax.experimental.pallas import tpu as pltpu

F32 = jnp.float32
BF16 = jnp.bfloat16

A_HEADS = 8
A_KV_HEADS = 2
A_HEAD_DIM = 64
A_WINDOW = 128
ROPE_THETA = 500000.0
A_ROPE_DIM = A_HEAD_DIM // 4
B_CONV = 31
C_HEADS = 16
C_HEAD_DIM = 64
C_GROUPS = 2
C_STATE = 128
C_CONV = 4
C_CHUNK = 128
D_HEADS = 4
D_QK_DIM = 128
D_V_DIM = 256
D_CHUNK = 128
RET_THETA = 10000.0
NORM_EPS = 1e-5

LANES = 128
SUBLANES = 8
VMEM_LIMIT = 56 * 1024 * 1024

ROW_TILE = 512
ATTN_TILE = 512
CONV_ROWS = 32
CONV_HALO = 32
FFN_CHUNK = 256
PROJ_COLS = 256
CONV4_ROWS = 256
SCAN_TILE = 512


def _cparams(*sem):
    return pltpu.CompilerParams(dimension_semantics=sem, vmem_limit_bytes=VMEM_LIMIT)


def _resident(shape):
    nd = len(shape)
    return pl.BlockSpec(shape, lambda *_: (0,) * nd, pipeline_mode=pl.Buffered(1))


def _resident_layer(shape, layer):
    return pl.BlockSpec((None,) + tuple(shape[1:]), lambda *_: (layer, 0, 0), pipeline_mode=pl.Buffered(1))


def _layer_norm(y, g, b):
    mu = jnp.mean(y, axis=-1, keepdims=True)
    d = y - mu
    var = jnp.mean(d * d, axis=-1, keepdims=True)
    return d * lax.rsqrt(var + NORM_EPS) * g + b


def _silu(v):
    h = 0.5 * v
    return h + h * jnp.tanh(h)


def _dot(a, b):
    return jnp.dot(a, b, preferred_element_type=F32)


def _dot_nt(a, b):
    return lax.dot_general(a, b, (((1,), (1,)), ((), ())), preferred_element_type=F32)


def _dot_tn(a, b):
    return lax.dot_general(a, b, (((0,), (0,)), ((), ())), preferred_element_type=F32)


def _split_bf16(v, parts):
    out = []
    r = v
    for _ in range(parts):
        p = r.astype(BF16)
        out.append(p)
        r = r - p.astype(F32)
    return out


def _even_in_kernel(nseq, x_ref, wq_ref, wkv_ref, wu_ref, cos_ref, sa_ref, sb_ref,
                    cw_ref, cb_ref, cg_ref, cbeta_ref,
                    q_ref, k_ref, v_ref, yc_ref, var_ref, halo_ref):
    i = pl.program_id(0)
    tm = x_ref.shape[0]
    xb = x_ref[...].astype(BF16)
    c, sa, sb = cos_ref[...], sa_ref[...], sb_ref[...]
    rot_dim = A_ROPE_DIM // 2

    def rot(t):
        return t * c + pltpu.roll(t, rot_dim, 1) * sa + pltpu.roll(t, LANES - rot_dim, 1) * sb

    @pl.when(i % nseq == 0)
    def _():
        halo_ref[...] = jnp.zeros_like(halo_ref)

    C = wu_ref.shape[1] // 2
    ext_rows = CONV_HALO + tm
    for p0 in range(0, C, PROJ_COLS):
        ua = _dot(xb, wu_ref[:, p0:p0 + PROJ_COLS])
        ug = _dot(xb, wu_ref[:, C + p0:C + p0 + PROJ_COLS])
        for c0 in range(p0, p0 + PROJ_COLS, LANES):
            cols = slice(c0, c0 + LANES)
            hglu = ua[:, c0 - p0:c0 - p0 + LANES] * jax.nn.sigmoid(ug[:, c0 - p0:c0 - p0 + LANES])
            ext = jnp.concatenate([halo_ref[:, cols], hglu], axis=0)
            halo_ref[:, cols] = hglu[tm - CONV_HALO:, :]
            var_ref[0, :, :, cols] = ext.reshape(ext_rows // SUBLANES, SUBLANES, LANES)
            for r in range(1, SUBLANES):
                sh = pltpu.roll(ext, ext_rows - r, 0)
                var_ref[r, :, :, cols] = sh.reshape(ext_rows // SUBLANES, SUBLANES, LANES)

    def conv_pieces():
        first = CONV_HALO - (B_CONV - 1)
        nt = CONV_ROWS // SUBLANES
        for r0 in range(0, tm, CONV_ROWS):
            acc = None
            for kk in range(B_CONV):
                d = first + kk
                s = r0 // SUBLANES + d // SUBLANES
                t = cw_ref[kk] * var_ref[d % SUBLANES, s:s + nt]
                acc = t + cb_ref[...] if acc is None else acc + t
            y = _layer_norm(acc, cg_ref[...], cbeta_ref[...])
            yc_ref[r0:r0 + CONV_ROWS, :] = _silu(y).reshape(CONV_ROWS, C).astype(BF16)
            yield

    def proj_pieces():
        qscale = A_HEAD_DIM ** -0.5
        for c0 in range(0, wq_ref.shape[1], PROJ_COLS):
            q = _dot(xb, wq_ref[:, c0:c0 + PROJ_COLS])
            for j in range(PROJ_COLS // LANES):
                q_ref[:, c0 + j * LANES:c0 + (j + 1) * LANES] = (
                    rot(q[:, j * LANES:(j + 1) * LANES]) * qscale).astype(BF16)
            yield
        kv = _dot(xb, wkv_ref[...])
        k2 = rot(kv[:, :LANES])
        v2 = kv[:, LANES:]
        k2s = pltpu.roll(k2, A_HEAD_DIM, 1)
        v2s = pltpu.roll(v2, A_HEAD_DIM, 1)
        lo = lax.broadcasted_iota(jnp.int32, k2.shape, 1) < A_HEAD_DIM
        zero = jnp.zeros_like(v2)
        k_ref[:, :LANES] = jnp.where(lo, k2, k2s).astype(BF16)
        k_ref[:, LANES:] = jnp.where(lo, k2s, k2).astype(BF16)
        for j, blk in enumerate((jnp.where(lo, v2, zero), jnp.where(lo, zero, v2s),
                                 jnp.where(lo, v2s, zero), jnp.where(lo, zero, v2))):
            v_ref[:, j * LANES:(j + 1) * LANES] = blk.astype(BF16)
        yield

    conv, proj = conv_pieces(), proj_pieces()
    per = max(1, (tm // CONV_ROWS) // 4)
    while True:
        done = [next(conv, "done") for _ in range(per)][-1] == "done"
        if next(proj, "done") == "done" and done:
            break


def _even_in(x2, wq, wkv, wu, cos_t, sa_t, sb_t, cw, cb, cg, cbeta, seq):
    T, D = x2.shape
    assert wkv.shape[1] == 2 * LANES, "k and v must fill exactly 128 lanes each"
    tm = min(ROW_TILE, seq)
    nseq = seq // tm
    C = wu.shape[1] // 2
    row = lambda i: (i, 0)
    tab = pl.BlockSpec((tm, LANES), lambda i: (i % nseq, 0))
    consts = (cw, cb, cg, cbeta)
    widths = (wq.shape[1], 2 * LANES, 4 * LANES, C)
    return pl.pallas_call(
        functools.partial(_even_in_kernel, nseq),
        grid=(T // tm,),
        in_specs=[pl.BlockSpec((tm, D), row), _resident(wq.shape), _resident(wkv.shape), _resident(wu.shape),
                  tab, tab, tab] + [_resident(t.shape) for t in consts],
        out_specs=[pl.BlockSpec((tm, n), row) for n in widths],
        out_shape=[jax.ShapeDtypeStruct((T, n), BF16) for n in widths],
        scratch_shapes=[pltpu.VMEM((SUBLANES, (CONV_HALO + tm) // SUBLANES, SUBLANES, C), F32),
                        pltpu.VMEM((CONV_HALO, C), F32)],
        compiler_params=_cparams("arbitrary"),
        name="even_in_conv",
    )(x2, wq, wkv, wu, cos_t, sa_t, sb_t, *consts)


def _attn_kernel(sink_ref, q_ref, kp_ref, kc_ref, vp_ref, vc_ref, o_ref):
    n = pl.program_id(1)
    W = A_WINDOW
    G = A_HEADS // A_KV_HEADS
    nsub = q_ref.shape[0] // W
    r = lax.broadcasted_iota(jnp.int32, (G * W, 2 * W), 0)
    i = jnp.bitwise_and(r, W - 1)
    c = lax.broadcasted_iota(jnp.int32, (G * W, 2 * W), 1)
    band = jnp.logical_and(c > i, c <= i + W)
    neg = jnp.float32(-jnp.inf)
    bias_inner = jnp.where(band, 0.0, neg)
    bias_first = jnp.where(jnp.logical_and(band, jnp.logical_or(c >= W, n > 0)), 0.0, neg)
    rg = lax.broadcasted_iota(jnp.int32, (G * W, 1), 0) // W
    lane = lax.broadcasted_iota(jnp.int32, (W, LANES), 1)
    lo = lane < A_HEAD_DIM
    ones = jnp.ones((2 * W, LANES), BF16)
    for sub in range(nsub):
        rows = slice(sub * W, (sub + 1) * W)
        prow = slice((sub - 1) * W, sub * W)
        bias = bias_first if sub == 0 else bias_inner
        for h in range(A_KV_HEADS):
            kl = slice(h * LANES, (h + 1) * LANES)
            kprev = kp_ref[:, kl] if sub == 0 else kc_ref[prow, kl]
            kcat = jnp.concatenate([kprev, kc_ref[rows, kl]], axis=0)
            parts = []
            for jj in range(G // 2):
                j = h * (G // 2) + jj
                q2 = q_ref[rows, j * LANES:(j + 1) * LANES]
                parts.append(jnp.where(lo, q2, jnp.zeros_like(q2)))
                parts.append(jnp.where(lo, jnp.zeros_like(q2), q2))
            s = _dot_nt(jnp.concatenate(parts, axis=0), kcat) + bias
            sink = jnp.full((G * W, 1), sink_ref[h * G], F32)
            for gi in range(1, G):
                sink = jnp.where(rg == gi, sink_ref[h * G + gi], sink)
            m = jnp.maximum(jnp.max(jnp.maximum(s[:, :W], s[:, W:]), axis=-1, keepdims=True), sink)
            p = jnp.exp(s - m).astype(BF16)
            esink = jnp.exp(sink - m)
            vaug = []
            for half in range(2):
                vl = slice((2 * h + half) * LANES, (2 * h + half + 1) * LANES)
                vprev = vp_ref[:, vl] if sub == 0 else vc_ref[prow, vl]
                vcat = jnp.concatenate([vprev, vc_ref[rows, vl]], axis=0)
                vaug.append(jnp.concatenate([vcat, ones], axis=1))
            for jj in range(G // 2):
                j = h * (G // 2) + jj
                he = slice((2 * jj) * W, (2 * jj + 1) * W)
                ho = slice((2 * jj + 1) * W, (2 * jj + 2) * W)
                re = _dot(p[he], vaug[0])
                ro = _dot(p[ho], vaug[1])
                acc = re[:, :LANES] + ro[:, :LANES]
                d = jnp.where(lo, re[:, LANES:] + esink[he], ro[:, LANES:] + esink[ho])
                o_ref[rows, j * LANES:(j + 1) * LANES] = (acc * (1.0 / d)).astype(BF16)


def _attention(sinks, q, k, v, batch, seq):
    W = A_WINDOW
    tq = min(ATTN_TILE, seq)
    per = tq // W
    nt = seq // tq
    cur = lambda b, n: (b * nt + n, 0)
    prev = lambda b, n: (jnp.maximum((b * nt + n) * per - 1, 0), 0)
    return pl.pallas_call(
        _attn_kernel,
        grid=(batch, nt),
        in_specs=[pl.BlockSpec(memory_space=pltpu.SMEM),
                  pl.BlockSpec((tq, q.shape[1]), cur),
                  pl.BlockSpec((W, k.shape[1]), prev), pl.BlockSpec((tq, k.shape[1]), cur),
                  pl.BlockSpec((W, v.shape[1]), prev), pl.BlockSpec((tq, v.shape[1]), cur)],
        out_specs=pl.BlockSpec((tq, q.shape[1]), cur),
        out_shape=jax.ShapeDtypeStruct(q.shape, BF16),
        compiler_params=_cparams("parallel", "parallel"),
        name="swa_attention",
    )(sinks, q, k, k, v, v)


def _out_ffn_kernel(alpha, ya_ref, yb_ref, x_ref, wa_ref, wb_ref, g1_ref, b1_ref,
                    wg_ref, wu_ref, wd_ref, g2_ref, b2_ref, o_ref, x1_ref, xb_ref):
    mix = _dot(ya_ref[...], wa_ref[...]) + _dot(yb_ref[...], wb_ref[...])
    x1 = _layer_norm(alpha * x_ref[...] + mix, g1_ref[...], b1_ref[...])
    x1_ref[...] = x1
    xb_ref[...] = x1.astype(BF16)
    hidden = wg_ref.shape[1]
    acc = None
    for c0 in range(0, hidden, FFN_CHUNK):
        c1 = min(c0 + FFN_CHUNK, hidden)
        gate = _dot(xb_ref[...], wg_ref[:, c0:c1])
        up = _dot(xb_ref[...], wu_ref[:, c0:c1])
        t = _dot((_silu(gate) * up).astype(BF16), wd_ref[c0:c1, :])
        acc = t if acc is None else acc + t
    o_ref[...] = _layer_norm(alpha * x1_ref[...] + acc, g2_ref[...], b2_ref[...])


def _out_ffn(ya, yb, x2, wa, wb, g1, b1, wg, wu, wd, g2, b2, layer, alpha):
    T, D = x2.shape
    tm = min(ROW_TILE, T)
    row = lambda i: (i, 0)
    consts = (wa, wb, g1, b1, wg, wu, wd, g2, b2)
    const_specs = [_resident_layer(t.shape, layer) if t.ndim == 3 else _resident(t.shape) for t in consts]
    return pl.pallas_call(
        functools.partial(_out_ffn_kernel, alpha),
        grid=(T // tm,),
        in_specs=[pl.BlockSpec((tm, ya.shape[1]), row), pl.BlockSpec((tm, yb.shape[1]), row),
                  pl.BlockSpec((tm, D), row)] + const_specs,
        out_specs=pl.BlockSpec((tm, D), row),
        out_shape=jax.ShapeDtypeStruct((T, D), F32),
        scratch_shapes=[pltpu.VMEM((tm, D), F32), pltpu.VMEM((tm, D), BF16)],
        compiler_params=_cparams("parallel"),
        name="out_ffn",
    )(ya, yb, x2, *consts)


def _odd_in_kernel(nseq, x_ref, wz_ref, wx_ref, wdt_ref, wq_ref, wk_ref, wv_ref, wg_ref, cos_ref, sin_ref,
                   cw_ref, cb_ref, dtb_ref,
                   z_ref, xs_ref, bc_ref, dt_ref, q_ref, k_ref, v_ref, g_ref, buf_ref):
    i = pl.program_id(0)
    tm = x_ref.shape[0]
    inner = xs_ref.shape[1]
    xb = x_ref[...].astype(BF16)

    def col_blocks(w_ref):
        for c0 in range(0, w_ref.shape[1], PROJ_COLS):
            c1 = min(c0 + PROJ_COLS, w_ref.shape[1])
            yield c0, c1, _dot(xb, w_ref[:, c0:c1])

    @pl.when(i % nseq == 0)
    def _():
        buf_ref[0:SUBLANES, :] = jnp.zeros((SUBLANES, buf_ref.shape[1]), F32)

    @pl.when(i % nseq != 0)
    def _():
        buf_ref[0:SUBLANES, :] = buf_ref[tm:tm + SUBLANES, :]

    rblk = min(CONV4_ROWS, tm)
    ntile = (SUBLANES + rblk) // SUBLANES

    def conv_block(r0, c0):
        cols = slice(c0, c0 + LANES)
        ext = buf_ref[r0:r0 + SUBLANES + rblk, cols]
        acc = cb_ref[:, cols] + cw_ref[C_CONV - 1, :, cols] * ext.reshape(ntile, SUBLANES, LANES)
        for kk in range(1, C_CONV):
            sh = pltpu.roll(ext, kk, 0).reshape(ntile, SUBLANES, LANES)
            acc = acc + cw_ref[C_CONV - 1 - kk, :, cols] * sh
        y = _silu(acc[1:]).reshape(rblk, LANES)
        if c0 < inner:
            xs_ref[r0:r0 + rblk, cols] = y
        else:
            bc_ref[r0:r0 + rblk, c0 - inner:c0 - inner + LANES] = y.astype(BF16)

    for c0, c1, r in col_blocks(wx_ref):
        buf_ref[SUBLANES:SUBLANES + tm, c0:c1] = r
        for cc in range(c0, c1, LANES):
            for r0 in range(0, tm, rblk):
                conv_block(r0, cc)

    for c0, c1, r in col_blocks(wz_ref):
        z_ref[:, c0:c1] = _silu(r)

    c, s = cos_ref[...], sin_ref[...]
    kscale = D_QK_DIM ** -0.5

    def rot(t):
        return t * c + pltpu.roll(t, D_QK_DIM // 2, 1) * s

    for c0, c1, r in col_blocks(wq_ref):
        for cc in range(c0, c1, D_QK_DIM):
            q_ref[:, cc:cc + D_QK_DIM] = rot(r[:, cc - c0:cc - c0 + D_QK_DIM]).astype(BF16)
    for c0, c1, r in col_blocks(wk_ref):
        for cc in range(c0, c1, D_QK_DIM):
            k_ref[:, cc:cc + D_QK_DIM] = rot(r[:, cc - c0:cc - c0 + D_QK_DIM]) * kscale
    for c0, c1, r in col_blocks(wv_ref):
        v_ref[:, c0:c1] = r.astype(BF16)
    for c0, c1, r in col_blocks(wg_ref):
        g_ref[:, c0:c1] = _silu(r)
    dtv = _dot(xb, wdt_ref[...]) + dtb_ref[...]
    dt_ref[...] = jnp.maximum(dtv, 0.0) + jnp.log1p(jnp.exp(-jnp.abs(dtv)))


def _odd_in(x2, wz, wx, wdt, wq, wk, wv, wg, cos_t, sin_t, cw, cb, dtb, seq):
    T, D = x2.shape
    tm = min(ROW_TILE, seq)
    nseq = seq // tm
    inner = wz.shape[1]
    row = lambda i: (i, 0)
    tab = pl.BlockSpec((tm, LANES), lambda i: (i % nseq, 0))
    ws = (wz, wx, wdt, wq, wk, wv, wg)
    consts = (cw, cb, dtb)
    widths = (inner, inner, wx.shape[1] - inner, wdt.shape[1], wq.shape[1], wk.shape[1], wv.shape[1], wg.shape[1])
    dts = (F32, F32, BF16, F32, BF16, F32, BF16, F32)
    return pl.pallas_call(
        functools.partial(_odd_in_kernel, nseq),
        grid=(T // tm,),
        in_specs=[pl.BlockSpec((tm, D), row)] + [_resident(w.shape) for w in ws] + [tab, tab]
                 + [_resident(t.shape) for t in consts],
        out_specs=[pl.BlockSpec((tm, n), row) for n in widths],
        out_shape=[jax.ShapeDtypeStruct((T, n), dt) for n, dt in zip(widths, dts)],
        scratch_shapes=[pltpu.VMEM((SUBLANES + tm, wx.shape[1]), F32)],
        compiler_params=_cparams("arbitrary"),
        name="odd_in",
    )(x2, *ws, cos_t, sin_t, *consts)


def _ssd_tile(z_ref, xs_ref, bc_ref, dt_ref, alog_ref, dskip_ref, ng_ref,
              expand_ref, tri_ref, o_ref, state_ref, y_ref):
    L = C_CHUNK
    nchunk = z_ref.shape[0] // L
    dt = dt_ref[...]
    dta = dt * (-jnp.exp(alog_ref[...]))
    acs_c = [_cumsum_rows(dta[c * L:(c + 1) * L], tri_ref[...]) for c in range(nchunk)]
    acs = jnp.concatenate(acs_c, axis=0)
    last = jnp.concatenate([jnp.broadcast_to(t[L - 1:L, :], t.shape) for t in acs_c], axis=0)
    to_state_e = _expand_heads(dt * jnp.exp(last - acs), expand_ref[...])
    from_start_e = _expand_heads(jnp.exp(acs), expand_ref[...])
    yield
    for c in range(nchunk):
        rows = slice(c * L, (c + 1) * L)
        yield from _ssd_chunk(z_ref.at[rows], xs_ref.at[rows], bc_ref.at[rows], acs_c[c], dt[rows],
                              to_state_e[rows], from_start_e[rows], dskip_ref, ng_ref, o_ref.at[rows],
                              state_ref, y_ref)


def _ssd_chunk(z_ref, xs_ref, bc_ref, acs, dt, to_state_e, from_start_e, dskip_ref, ng_ref,
               o_ref, state_ref, y_ref):
    L = C_CHUNK
    gw = C_STATE
    hpg = C_HEADS // C_GROUPS
    gcols = hpg * C_HEAD_DIM
    xs = xs_ref[...]
    xs_b = xs.astype(BF16)
    acs_t = acs.T
    dt_t = dt.T
    cdec_e = from_start_e[L - 1:L, :]
    wst = (xs * to_state_e).astype(BF16)

    row = lax.broadcasted_iota(jnp.int32, (L, L), 0)
    col = lax.broadcasted_iota(jnp.int32, (L, L), 1)
    causal = col <= row
    lane = lax.broadcasted_iota(jnp.int32, (L, LANES), 1)
    lo = lane < C_HEAD_DIM
    zero_b = jnp.zeros((L, LANES), BF16)
    quad = 2 * LANES

    for g in range(C_GROUPS):
        bm = bc_ref[:, g * gw:(g + 1) * gw]
        cm = bc_ref[:, (C_GROUPS + g) * gw:(C_GROUPS + g + 1) * gw]
        cbm = _dot_nt(cm, bm)
        gs = slice(g * gcols, (g + 1) * gcols)
        prev = state_ref[:, gs]
        y_off = _dot(cm, prev.astype(BF16)) * from_start_e[:, gs]
        new = _dot_tn(bm, wst[:, gs])
        state_ref[:, gs] = prev * cdec_e[:, gs] + new
        for qd in range(gcols // quad):
            ms, xr = [], []
            for jp in range(2):
                c0 = g * gcols + qd * quad + jp * LANES
                xp = xs_b[:, c0:c0 + LANES]
                for half in range(2):
                    h = (c0 // C_HEAD_DIM) + half
                    seg = acs[:, h:h + 1] - acs_t[h:h + 1, :]
                    dec = jnp.exp(jnp.where(causal, seg, -jnp.inf))
                    ms.append((cbm * dec * dt_t[h:h + 1, :]).astype(BF16))
                    xh = jnp.where(lo if half == 0 else jnp.logical_not(lo), xp, zero_b)
                    xr.append(jnp.concatenate([xh, zero_b] if jp == 0 else [zero_b, xh], axis=1))
            cols = slice(g * gcols + qd * quad, g * gcols + (qd + 1) * quad)
            y = (_dot(jnp.concatenate(ms, axis=1), jnp.concatenate(xr, axis=0))
                 + y_off[:, qd * quad:(qd + 1) * quad] + xs[:, cols] * dskip_ref[:, cols])
            y_ref[:, cols] = y * z_ref[:, cols]
            yield

    for g in range(C_GROUPS):
        gs = slice(g * gcols, (g + 1) * gcols)
        yg = y_ref[:, gs]
        ms = jnp.mean(yg * yg, axis=-1, keepdims=True)
        o_ref[:, gs] = (yg * lax.rsqrt(ms + NORM_EPS) * ng_ref[:, gs]).astype(BF16)
    yield


def _expand_heads(v, expand3):
    lane = lax.broadcasted_iota(jnp.int32, v.shape, 1)
    hi, mid, low = _split_bf16(jnp.where(lane < C_HEADS, v, 0.0), 3)
    packed = (hi.astype(F32) + pltpu.roll(mid.astype(F32), C_HEADS, 1)
              + pltpu.roll(low.astype(F32), 2 * C_HEADS, 1))
    return _dot(packed.astype(BF16), expand3)


def _cumsum_rows(v, tri):
    acc = None
    for p in _split_bf16(v, 3):
        t = _dot(tri, p)
        acc = t if acc is None else acc + t
    return acc


def _ret_tile(q_ref, k_ref, v_ref, g_ref, dmat_ref, toend_ref, fstart_ref, cdec_ref, gng_ref, gnb_ref,
              o_ref, state_ref):
    L = D_CHUNK
    for r0 in range(0, q_ref.shape[0], L):
        rows = slice(r0, r0 + L)
        for h in range(D_HEADS):
            ks = slice(h * D_QK_DIM, (h + 1) * D_QK_DIM)
            vs = slice(h * D_V_DIM, (h + 1) * D_V_DIM)
            qh = q_ref[rows, ks]
            kh = k_ref[rows, ks]
            vh = v_ref[rows, vs]
            scores = _dot_nt(qh, kh.astype(BF16)) * dmat_ref[h]
            prev = state_ref[:, vs]
            y = _dot(scores.astype(BF16), vh) + _dot(qh, prev.astype(BF16)) * fstart_ref[:, vs]
            kd = (kh * toend_ref[:, ks]).astype(BF16)
            state_ref[:, vs] = prev * cdec_ref[:, vs] + _dot_tn(kd, vh)
            mu = jnp.mean(y, axis=-1, keepdims=True)
            d = y - mu
            var = jnp.mean(d * d, axis=-1, keepdims=True)
            yn = d * lax.rsqrt(var + NORM_EPS) * gng_ref[:, vs] + gnb_ref[:, vs]
            o_ref[rows, vs] = (g_ref[rows, vs] * yn).astype(BF16)
            yield


N_SSD_TILED = 4
N_SSD_CONST = 5
N_RET_TILED = 4
N_RET_CONST = 6


def _scan_kernel(*refs):
    n_ssd = N_SSD_TILED + N_SSD_CONST
    n_in = n_ssd + N_RET_TILED + N_RET_CONST
    ssd_in, ret_in = refs[:n_ssd], refs[n_ssd:n_in]
    o_ssd, o_ret, ssd_state, y_ref, ret_state = refs[n_in:]

    @pl.when(pl.program_id(1) == 0)
    def _():
        ssd_state[...] = jnp.zeros_like(ssd_state)
        ret_state[...] = jnp.zeros_like(ret_state)

    pending = [_ssd_tile(*ssd_in, o_ssd, ssd_state, y_ref), _ret_tile(*ret_in, o_ret, ret_state)]
    while pending:
        for gen in list(pending):
            if next(gen, "done") == "done":
                pending.remove(gen)


def _scans(ssd_tiled, ssd_consts, ret_tiled, ret_consts, batch, seq):
    tr = min(SCAN_TILE, seq)
    nt = seq // tr
    cur = lambda b, c: (b * nt + c, 0)
    tiled = lambda ts: [pl.BlockSpec((tr, t.shape[1]), cur) for t in ts]
    const = lambda ts: [_resident(t.shape) for t in ts]
    z, v = ssd_tiled[0], ret_tiled[2]
    return pl.pallas_call(
        _scan_kernel,
        grid=(batch, nt),
        in_specs=tiled(ssd_tiled) + const(ssd_consts) + tiled(ret_tiled) + const(ret_consts),
        out_specs=[pl.BlockSpec((tr, z.shape[1]), cur), pl.BlockSpec((tr, v.shape[1]), cur)],
        out_shape=[jax.ShapeDtypeStruct(z.shape, BF16), jax.ShapeDtypeStruct(v.shape, BF16)],
        scratch_shapes=[pltpu.VMEM((C_STATE, z.shape[1]), F32),
                        pltpu.VMEM((C_CHUNK, z.shape[1]), F32),
                        pltpu.VMEM((D_QK_DIM, v.shape[1]), F32)],
        compiler_params=_cparams("arbitrary", "arbitrary"),
        name="ssd_retention_scan",
    )(*ssd_tiled, *ssd_consts, *ret_tiled, *ret_consts)


def _even_tables(seq):
    half = A_ROPE_DIM // 2
    inv = jnp.power(jnp.float32(ROPE_THETA), -jnp.arange(half, dtype=F32) / half)
    ang = jnp.arange(seq, dtype=F32)[:, None] * inv[None, :]
    cos, sin = jnp.cos(ang), jnp.sin(ang)
    ones = jnp.ones((seq, A_HEAD_DIM - 2 * half), F32)
    zeros = jnp.zeros((seq, A_HEAD_DIM - 2 * half), F32)
    zh = jnp.zeros((seq, half), F32)
    c = jnp.concatenate([cos, cos, ones], axis=1)
    sa = jnp.concatenate([zh, sin, zeros], axis=1)
    sb = jnp.concatenate([-sin, zh, zeros], axis=1)
    rep = LANES // A_HEAD_DIM
    return jnp.tile(c, (1, rep)), jnp.tile(sa, (1, rep)), jnp.tile(sb, (1, rep))


def _ret_tables(seq):
    half = D_QK_DIM // 2
    inv = 1.0 / jnp.power(jnp.float32(RET_THETA), jnp.linspace(0.0, 1.0, half, dtype=F32))
    ang = jnp.arange(seq, dtype=F32)[:, None] * inv[None, :]
    cos, sin = jnp.cos(ang), jnp.sin(ang)
    L = D_CHUNK
    log_gamma = jnp.log(1.0 - jnp.power(2.0, -5.0 - jnp.arange(D_HEADS, dtype=F32)))
    idx = jnp.arange(L, dtype=F32)
    rel = idx[:, None] - idx[None, :]
    dmat = jnp.where(rel[None] >= 0, jnp.exp(log_gamma[:, None, None] * jnp.maximum(rel, 0.0)[None]), 0.0)
    to_end = jnp.exp(log_gamma[None, :] * (L - 1 - idx)[:, None])
    from_start = jnp.exp(log_gamma[None, :] * (idx + 1.0)[:, None])
    chunk_decay = jnp.exp(log_gamma * L)
    return (jnp.concatenate([cos, cos], axis=1), jnp.concatenate([-sin, sin], axis=1), dmat,
            jnp.repeat(to_end, D_QK_DIM, axis=1), jnp.repeat(from_start, D_V_DIM, axis=1),
            jnp.repeat(chunk_decay, D_V_DIM)[None, :])


def _row(v):
    return v.astype(F32)[None, :]


def _sublane_rep(v):
    v = v.astype(F32)
    return jnp.broadcast_to(v[..., None, :], v.shape[:-1] + (SUBLANES, v.shape[-1]))


def _pad_lanes(v, fill=0.0):
    pad = (-v.shape[-1]) % LANES
    return jnp.pad(v, [(0, 0)] * (v.ndim - 1) + [(0, pad)], constant_values=fill)


def _even_mixer(x2, w_in, sinks, dw_w, dw_b, cn_g, cn_b, w_out, tables, batch, seq):
    nq = A_HEADS * A_HEAD_DIM
    nkv = A_KV_HEADS * A_HEAD_DIM
    wq = w_in[:, :nq].astype(BF16)
    wkv = w_in[:, nq:nq + 2 * nkv].astype(BF16)
    wu = w_in[:, nq + 2 * nkv:].astype(BF16)
    q, k, v, y_conv = _even_in(x2, wq, wkv, wu, *tables, _sublane_rep(dw_w), _sublane_rep(dw_b),
                               _sublane_rep(cn_g), _sublane_rep(cn_b), seq)
    y_attn = _attention(sinks.astype(F32), q, k, v, batch, seq)
    wo = w_out.astype(BF16)
    return y_attn, y_conv, wo[:nq], wo[nq:]


def _odd_mixer(x2, w_in, conv_w, conv_b, dt_bias, a_log, d_skip, ssm_norm_g, gn_g, gn_b, w_out,
               tables, consts, batch, seq):
    inner = C_HEADS * C_HEAD_DIM
    xbc_w = inner + 2 * C_GROUPS * C_STATE
    nqk = D_HEADS * D_QK_DIM
    nv = D_HEADS * D_V_DIM
    sizes = (inner, xbc_w, C_HEADS, nqk, nqk, nv, nv)
    offs = [0]
    for s in sizes:
        offs.append(offs[-1] + s)
    wb = w_in.astype(BF16)
    wz, wx, wdt, wq, wk, wv, wg = [wb[:, offs[i]:offs[i + 1]] for i in range(len(sizes))]
    wdt = _pad_lanes(wdt)
    cos_t, sin_t, dmat, toend, fstart, cdec = tables
    expand, tri = consts
    z, xs, bc, dt, rq, rk, rv, rg = _odd_in(x2, wz, wx, wdt, wq, wk, wv, wg, cos_t, sin_t,
                                            _sublane_rep(conv_w), _sublane_rep(conv_b),
                                            _pad_lanes(_row(dt_bias)), seq)
    ssd_consts = (_pad_lanes(_row(a_log)), _row(jnp.repeat(d_skip.astype(F32), C_HEAD_DIM)), _row(ssm_norm_g),
                  expand, tri)
    ret_consts = (dmat, toend, fstart, cdec, _row(gn_g), _row(gn_b))
    y_ssm, y_ret = _scans((z, xs, bc, dt), ssd_consts, (rq, rk, rv, rg), ret_consts, batch, seq)
    wo = w_out.astype(BF16)
    return y_ssm, y_ret, wo[:inner], wo[inner:]


def _ssd_consts():
    k = jnp.arange(LANES)[:, None]
    cols = jnp.arange(C_HEADS * C_HEAD_DIM)[None, :] // C_HEAD_DIM
    expand = jnp.logical_and(k % C_HEADS == cols, k < 3 * C_HEADS).astype(BF16)
    i = jnp.arange(C_CHUNK)
    tri = (i[None, :] <= i[:, None]).astype(BF16)
    return expand, tri


def kernel(x, ln_mix_g, ln_mix_b, ln_ffn_g, ln_ffn_b, ffn_w_gate, ffn_w_up, ffn_w_down, ev_w_in, ev_sinks, ev_dw_w, ev_dw_b, ev_cn_g, ev_cn_b, ev_w_out, od_w_in, od_conv_w, od_conv_b, od_dt_bias, od_a_log, od_d_skip, od_ssm_norm_g, od_ret_gn_g, od_ret_gn_b, od_w_out):
    batch, seq, d_model = x.shape
    depth = ln_mix_g.shape[0]
    alpha = float((2 * depth) ** 0.25)
    ev_tables = _even_tables(seq)
    od_tables = _ret_tables(seq)
    od_consts = _ssd_consts()
    ffn_w = (ffn_w_gate.astype(BF16), ffn_w_up.astype(BF16), ffn_w_down.astype(BF16))
    x2 = x.reshape(batch * seq, d_model)
    for layer in range(depth):
        i = layer // 2
        if layer % 2 == 0:
            mixed = _even_mixer(x2, ev_w_in[i], ev_sinks[i], ev_dw_w[i], ev_dw_b[i], ev_cn_g[i], ev_cn_b[i],
                                ev_w_out[i], ev_tables, batch, seq)
        else:
            mixed = _odd_mixer(x2, od_w_in[i], od_conv_w[i], od_conv_b[i], od_dt_bias[i], od_a_log[i],
                               od_d_skip[i], od_ssm_norm_g[i], od_ret_gn_g[i], od_ret_gn_b[i], od_w_out[i],
                               od_tables, od_consts, batch, seq)
        x2 = _mixer_out_ffn(mixed, x2, ln_mix_g[layer], ln_mix_b[layer], ffn_w, layer,
                            ln_ffn_g[layer], ln_ffn_b[layer], alpha)
    return x2.reshape(batch, seq, d_model)


def _mixer_out_ffn(mixed, x2, ln1_g, ln1_b, ffn_w, layer, ln2_g, ln2_b, alpha):
    ya, yb, wa, wb = mixed
    return _out_ffn(ya, yb, x2, wa, wb, _row(ln1_g), _row(ln1_b), *ffn_w, _row(ln2_g), _row(ln2_b), layer, alpha)
```
